```python
import jax
import jax.numpy as jnp
from jax import lax
import numpy as np

D_MODEL = 2048
BATCH = 4
SEQ = 2048
DEPTH = 4

CTX_LEN = 256
GRID_W = 64
MIX_W = D_MODEL
N_BRANCH = 3
NORM_EPS = 1e-6
RWKV_HEAD_DIM = 64
RWKV_HEADS = MIX_W // RWKV_HEAD_DIM
DECAY_LORA = 96
AICL_LORA = 96
SHIFT_K = 3
RWKV_GN_EPS = 64e-5
N_SHIFT = 3 * MIX_W + 2 * DECAY_LORA + 2 * AICL_LORA
N_A = N_SHIFT + MIX_W
HGRN_EXPAND = 128
HGRN_HEADS = MIX_W // HGRN_EXPAND
HGRN_HEAD_DIM = MIX_W // HGRN_HEADS
HGRN_CHUNK = 64
HGRN_NORM_EPS = 1e-5
N_B = 5 * MIX_W
NA_HEADS = 16
NA_HEAD_DIM = MIX_W // NA_HEADS
NA_KH = 8
NA_KW = 16
NA_QBLK = 16
NA_KBLK = 32
N_C = 4 * MIX_W
N_GATE = N_BRANCH * D_MODEL
N_IN = N_A + N_B + N_C + N_GATE

kernel_name = 'hybrid_rwkv7_hgrn2_natten_dit_block'


def rmsnorm(x, g, eps=NORM_EPS):
    xf = x.astype(jnp.float32)
    y = xf * lax.rsqrt(jnp.mean(xf * xf, axis=-1, keepdims=True) + eps)
    return (y * g.astype(jnp.float32)).astype(x.dtype)


def heads(t, n_heads):
    return t.reshape(t.shape[:-1] + (n_heads, t.shape[-1] // n_heads))


def centred_dwconv(u, w):
    pad = w.shape[0] // 2
    return lax.conv_general_dilated(u, w[:, None, :].astype(u.dtype), window_strides=(1,),
                                    padding=[(pad, pad)], dimension_numbers=('NWC', 'WIO', 'NWC'),
                                    feature_group_count=u.shape[-1])


def run_direction(scan_fn, arrs, s0, reverse, readout):
    if reverse:
        arrs = tuple(jnp.flip(a, axis=1) for a in arrs)
    o, s = scan_fn(*arrs, s0, readout)
    if reverse and readout:
        o = jnp.flip(o, axis=1)
    return o, s


def rwkv7_scan(r, w, k, v, kk, b, s0, readout):
    def step(s, inp):
        r_t, w_t, k_t, v_t, kk_t, b_t = inp
        sa = jnp.einsum('bhvk,bhk->bhv', s, kk_t)
        s = s * w_t[:, :, None, :] - sa[..., None] * b_t[:, :, None, :] + v_t[..., None] * k_t[:, :, None, :]
        o = jnp.einsum('bhvk,bhk->bhv', s, r_t) if readout else None
        return s, o
    xs = tuple(jnp.moveaxis(t, 1, 0) for t in (r, w, k, v, kk, b))
    s, o = lax.scan(step, s0, xs)
    return (jnp.moveaxis(o, 0, 1) if readout else None), s


def rwkv_prep(p, shift_w, w0, w_up, a0, a_up, k_k, k_a):
    f32 = jnp.float32
    u = centred_dwconv(p[..., :N_SHIFT], shift_w)
    gate = p[..., N_SHIFT:]
    r, k, v = (u[..., i * MIX_W:(i + 1) * MIX_W] for i in range(3))
    lo = u[..., 3 * MIX_W:]
    wd = lo[..., :2 * DECAY_LORA].reshape(lo.shape[:-1] + (2, DECAY_LORA))
    ad = lo[..., 2 * DECAY_LORA:].reshape(lo.shape[:-1] + (2, AICL_LORA))
    wl = (w0 + jnp.einsum('bnjr,jrc->bnjc', jnp.tanh(wd), w_up)).astype(f32)
    decay = jnp.exp(-jnp.exp(-jax.nn.softplus(-wl) - 0.5))
    a = jax.nn.sigmoid((a0 + jnp.einsum('bnjr,jrc->bnjc', ad, a_up)).astype(f32))
    kk = heads((k * k_k).astype(f32), RWKV_HEADS)
    kk = kk / jnp.maximum(jnp.sqrt(jnp.sum(kk * kk, axis=-1, keepdims=True)), 1e-12)
    kkw = kk.reshape(kk.shape[:-2] + (MIX_W,))
    kf = k.astype(f32)
    dirs = []
    for j in range(2):
        k_j = kf * (1.0 + (a[:, :, j] - 1.0) * k_a)
        b_j = kkw * a[:, :, j]
        dirs.append((heads(decay[:, :, j], RWKV_HEADS), heads(k_j, RWKV_HEADS), heads(b_j, RWKV_HEADS)))
    return heads(r.astype(f32), RWKV_HEADS), heads(v.astype(f32), RWKV_HEADS), kk, dirs, gate


def rwkv_post(o, r, dirs, v, gate, r_k, ln_g, ln_b):
    mu = jnp.mean(o, axis=-1, keepdims=True)
    var = jnp.mean(jnp.square(o - mu), axis=-1, keepdims=True)
    on = ((o - mu) * lax.rsqrt(var + RWKV_GN_EPS)).reshape(o.shape[:-2] + (MIX_W,)) * ln_g + ln_b
    bonus = sum(jnp.sum(r * k_j * r_k, axis=-1, keepdims=True) * v for (_, k_j, _) in dirs)
    y = (on + bonus.reshape(on.shape)) * jax.nn.silu(gate.astype(jnp.float32))
    return y.astype(gate.dtype)


def rwkv_branch(p_ctx, p_lat, shift_w, w0, w_up, a0, a_up, k_k, k_a, r_k, ln_g, ln_b, ctx_out):
    prm = (shift_w, w0, w_up, a0, a_up, k_k, k_a)
    rc, vc, kkc, dirs_c, gate_c = rwkv_prep(p_ctx, *prm)
    rl, vl, kkl, dirs_l, gate_l = rwkv_prep(p_lat, *prm)
    s0 = jnp.zeros((p_lat.shape[0], RWKV_HEADS, RWKV_HEAD_DIM, RWKV_HEAD_DIM), jnp.float32)
    outs_c, outs_l = [], []
    for j, rev in enumerate((False, True)):
        wc, kc, bc = dirs_c[j]
        wl, kl, bl = dirs_l[j]
        oc, sc = run_direction(rwkv7_scan, (rc, wc, kc, vc, kkc, bc), s0, rev, ctx_out)
        ol, _ = run_direction(rwkv7_scan, (rl, wl, kl, vl, kkl, bl), sc, rev, True)
        outs_c.append(oc)
        outs_l.append(ol)
    y_lat = rwkv_post(outs_l[0] + outs_l[1], rl, dirs_l, vl, gate_l, r_k, ln_g, ln_b)
    y_ctx = rwkv_post(outs_c[0] + outs_c[1], rc, dirs_c, vc, gate_c, r_k, ln_g, ln_b) if ctx_out else None
    return y_ctx, y_lat


def hgrn_lower_bounds(logits):
    p = jax.nn.softmax(logits.astype(jnp.float32), axis=0)
    return jnp.cumsum(p, axis=0) - p[0:1]


def gla_chunk_scan(q, k, v, g, s0, readout):
    bsz, n, h, _ = q.shape
    nc = n // HGRN_CHUNK

    def chunks(t):
        return t.reshape(bsz, nc, HGRN_CHUNK, h, t.shape[-1]).transpose(1, 0, 3, 2, 4)

    lower = jnp.tril(jnp.ones((HGRN_CHUNK, HGRN_CHUNK), dtype=bool))

    def step(s, inp):
        qc, kc, vc, gc = inp
        bcum = jnp.cumsum(gc, axis=2)
        blast = bcum[:, :, -1]
        s_new = jnp.exp(blast)[..., None] * s + jnp.einsum('bhjd,bhje->bhde', kc * jnp.exp(blast[:, :, None] - bcum), vc)
        if not readout:
            return s_new, None
        o_inter = jnp.einsum('bhid,bhde->bhie', qc * jnp.exp(bcum), s)
        decay = jnp.exp(jnp.where(lower[:, :, None], bcum[:, :, :, None, :] - bcum[:, :, None, :, :], -jnp.inf))
        att = jnp.einsum('bhid,bhjd,bhijd->bhij', qc, kc, decay)
        return s_new, o_inter + jnp.einsum('bhij,bhje->bhie', att, vc)

    s, o = lax.scan(step, s0, tuple(chunks(t) for t in (q, k, v, g)))
    if readout:
        o = o.transpose(1, 0, 3, 2, 4).reshape(bsz, n, h, o.shape[-1])
    return o, s


def hgrn_prep(p, lb):
    f32 = jnp.float32
    q, f_fwd, f_bwd, i, gate = jnp.split(p, 5, axis=-1)
    q = heads(jax.nn.silu(q.astype(f32)), HGRN_HEADS)
    v = heads(i.astype(f32), HGRN_HEADS)
    dirs = []
    for fr in (f_fwd, f_bwd):
        f = lb + (1.0 - lb) * jax.nn.sigmoid(fr.astype(f32))
        dirs.append((heads(1.0 - f, HGRN_HEADS), heads(jnp.log(f), HGRN_HEADS)))
    return q, v, dirs, gate


def hgrn_post(o, gate, norm_g):
    on = o * lax.rsqrt(jnp.mean(o * o, axis=-1, keepdims=True) + HGRN_NORM_EPS) * norm_g
    y = on.reshape(o.shape[:-2] + (MIX_W,)) * jax.nn.silu(gate.astype(jnp.float32))
    return y.astype(gate.dtype)


def hgrn_branch(p_ctx, p_lat, lb, norm_g, ctx_out):
    qc, vc, dirs_c, gate_c = hgrn_prep(p_ctx, lb)
    ql, vl, dirs_l, gate_l = hgrn_prep(p_lat, lb)
    s0 = jnp.zeros((p_lat.shape[0], HGRN_HEADS, HGRN_HEAD_DIM, HGRN_HEAD_DIM), jnp.float32)
    outs_c, outs_l = [], []
    for j, rev in enumerate((False, True)):
        kc, gc = dirs_c[j]
        kl, gl = dirs_l[j]
        oc, sc = run_direction(gla_chunk_scan, (qc, kc, vc, gc), s0, rev, ctx_out)
        ol, _ = run_direction(gla_chunk_scan, (ql, kl, vl, gl), sc, rev, True)
        outs_c.append(oc)
        outs_l.append(ol)
    y_lat = hgrn_post(outs_l[0] + outs_l[1], gate_l, norm_g)
    y_ctx = hgrn_post(outs_c[0] + outs_c[1], gate_c, norm_g) if ctx_out else None
    return y_ctx, y_lat


def na_col_tables():
    n_cb = GRID_W // NA_QBLK
    col = np.arange(GRID_W)
    cs = np.clip(col - NA_KW // 2, 0, GRID_W - NA_KW)
    kb = np.minimum(cs[::NA_QBLK], GRID_W - NA_KBLK)
    col_key = kb[:, None] + np.arange(NA_KBLK)
    cq = col.reshape(n_cb, NA_QBLK)
    csq = cs.reshape(n_cb, NA_QBLK)
    valid = (col_key[:, None, :] >= csq[..., None]) & (col_key[:, None, :] < csq[..., None] + NA_KW)
    off = np.clip(col_key[:, None, :] - cq[..., None] + NA_KW - 1, 0, 2 * NA_KW - 2)
    return col_key, valid, off


def na_branch(p_ctx, p_lat, rpb, ctx_out):
    f32 = jnp.float32
    scale = NA_HEAD_DIM ** -0.5

    def bhnd(t):
        return jnp.moveaxis(heads(t, NA_HEADS), 2, 1)

    q_c, k_c, v_c, gate_c = jnp.split(p_ctx, 4, axis=-1)
    q_c, k_c, v_c = bhnd(q_c), bhnd(k_c), bhnd(v_c)
    y_ctx = None
    if ctx_out:
        s = (jnp.einsum('bhqd,bhkd->bhqk', q_c, k_c) * scale).astype(f32)
        o = jnp.einsum('bhqk,bhkd->bhqd', jax.nn.softmax(s, axis=-1).astype(v_c.dtype), v_c)
        y_ctx = jnp.moveaxis(o, 1, 2).reshape(p_ctx.shape[:-1] + (MIX_W,)) * jax.nn.silu(gate_c)

    q_l, k_l, v_l, gate_l = jnp.split(p_lat, 4, axis=-1)
    bsz, L = p_lat.shape[:2]
    rows = L // GRID_W
    kh = min(NA_KH, rows)
    n_cb = GRID_W // NA_QBLK

    def grid(t):
        return bhnd(t).reshape(bsz, NA_HEADS, rows, GRID_W, NA_HEAD_DIM)

    qg, kg, vg = grid(q_l), grid(k_l), grid(v_l)
    col_key, col_valid, col_off = na_col_tables()

    def row_block(r):
        rs = jnp.clip(r - kh // 2, 0, rows - kh)
        q_blk = lax.dynamic_index_in_dim(qg, r, axis=2, keepdims=False).reshape(bsz, NA_HEADS, n_cb, NA_QBLK, NA_HEAD_DIM)
        k_blk = lax.dynamic_slice_in_dim(kg, rs, kh, axis=2)[:, :, :, col_key]
        v_blk = lax.dynamic_slice_in_dim(vg, rs, kh, axis=2)[:, :, :, col_key]
        dr = rs + jnp.arange(kh) - r + (NA_KH - 1)
        bias = jnp.take(rpb, dr, axis=1)[:, :, col_off].transpose(0, 2, 3, 1, 4)
        s_loc = (jnp.einsum('bhmqd,bhimld->bhmqil', q_blk, k_blk) * scale).astype(f32) + bias[None].astype(f32)
        s_loc = jnp.where(col_valid[:, :, None, :], s_loc, -jnp.inf)
        s_ctx = (jnp.einsum('bhmqd,bhnd->bhmqn', q_blk, k_c) * scale).astype(f32)
        s = jnp.concatenate([s_loc.reshape(bsz, NA_HEADS, n_cb, NA_QBLK, kh * NA_KBLK), s_ctx], axis=-1)
        p = jax.nn.softmax(s, axis=-1).astype(v_l.dtype)
        p_loc = p[..., :kh * NA_KBLK].reshape(bsz, NA_HEADS, n_cb, NA_QBLK, kh, NA_KBLK)
        o = jnp.einsum('bhmqil,bhimld->bhmqd', p_loc, v_blk) + jnp.einsum('bhmqn,bhnd->bhmqd', p[..., kh * NA_KBLK:], v_c)
        return o.reshape(bsz, NA_HEADS, GRID_W, NA_HEAD_DIM)

    o = lax.map(row_block, jnp.arange(rows))
    o = o.transpose(1, 0, 3, 2, 4).reshape(bsz, L, MIX_W)
    return y_ctx, o * jax.nn.silu(gate_l)


def merge(ya, yb, yc, gate_logits, w_branch, w_out):
    ga, gb, gc = jnp.split(jax.nn.sigmoid(gate_logits), 3, axis=-1)
    m = ga * (ya @ w_branch[0]) + gb * (yb @ w_branch[1]) + gc * (yc @ w_branch[2])
    return m @ w_out


def layer(x, xc, c, c_ctx, w_ada, b_ada, norm_g, w_in, rwkv_shift, rwkv_w0, rwkv_w_up, rwkv_a0, rwkv_a_up,
          rwkv_k_k, rwkv_k_a, rwkv_r_k, rwkv_ln_g, rwkv_ln_b, lb, hgrn_norm_g, na_rpb, w_branch, w_out, ctx_out):
    shift, scale, gate = jnp.split(jax.nn.silu(c) @ w_ada + b_ada, 3, axis=-1)
    shift_c, scale_c, gate_c = jnp.split(jax.nn.silu(c_ctx) @ w_ada + b_ada, 3, axis=-1)
    h = rmsnorm(x, norm_g) * (1.0 + scale[:, None]) + shift[:, None]
    hc = rmsnorm(xc, norm_g) * (1.0 + scale_c) + shift_c
    cuts = [N_A, N_A + N_B, N_A + N_B + N_C]
    pa, pb, pn, pg = jnp.split(h @ w_in, cuts, axis=-1)
    pa_c, pb_c, pn_c, pg_c = jnp.split(hc @ w_in, cuts, axis=-1)
    ya_c, ya = rwkv_branch(pa_c, pa, rwkv_shift, rwkv_w0, rwkv_w_up, rwkv_a0, rwkv_a_up, rwkv_k_k, rwkv_k_a,
                           rwkv_r_k, rwkv_ln_g, rwkv_ln_b, ctx_out)
    yb_c, yb = hgrn_branch(pb_c, pb, lb, hgrn_norm_g, ctx_out)
    yc_c, yc = na_branch(pn_c, pn, na_rpb, ctx_out)
    x = x + gate[:, None] * merge(ya, yb, yc, pg, w_branch, w_out)
    xc = xc + gate_c * merge(ya_c, yb_c, yc_c, pg_c, w_branch, w_out) if ctx_out else None
    return x, xc


def setup_inputs(seed: int = 0) -> dict:
    key = jax.random.key(seed)
    ks = jax.random.split(key, 24)
    f32 = jnp.float32

    def nrm(k, shape, s):
        return s * jax.random.normal(k, shape, f32)

    return {
        'x': nrm(ks[0], (BATCH, SEQ, D_MODEL), 1.0),
        'c': nrm(ks[1], (BATCH, D_MODEL), 1.0),
        'ctx': nrm(ks[2], (BATCH, CTX_LEN, D_MODEL), 1.0),
        'c_ctx': nrm(ks[3], (D_MODEL,), 1.0),
        'w_ada': nrm(ks[4], (DEPTH, D_MODEL, 3 * D_MODEL), 0.5 * D_MODEL ** -0.5),
        'b_ada': nrm(ks[5], (DEPTH, 3 * D_MODEL), 0.02),
        'norm_g': 1.0 + nrm(ks[6], (DEPTH, D_MODEL), 0.02),
        'w_in': nrm(ks[7], (DEPTH, D_MODEL, N_IN), D_MODEL ** -0.5),
        'rwkv_shift': jnp.array([0.25, 0.5, 0.25], f32)[None, :, None] + nrm(ks[8], (DEPTH, SHIFT_K, N_SHIFT), 0.05),
        'rwkv_w0': nrm(ks[9], (DEPTH, 2, MIX_W), 1.5) - 1.0,
        'rwkv_w_up': nrm(ks[10], (DEPTH, 2, DECAY_LORA, MIX_W), 0.5 * DECAY_LORA ** -0.5),
        'rwkv_a0': nrm(ks[11], (DEPTH, 2, MIX_W), 0.5),
        'rwkv_a_up': nrm(ks[12], (DEPTH, 2, AICL_LORA, MIX_W), 0.5 * AICL_LORA ** -0.5),
        'rwkv_k_k': 0.85 + nrm(ks[13], (DEPTH, MIX_W), 0.05),
        'rwkv_k_a': 1.0 + nrm(ks[14], (DEPTH, MIX_W), 0.05),
        'rwkv_r_k': nrm(ks[15], (DEPTH, RWKV_HEADS, RWKV_HEAD_DIM), 0.1),
        'rwkv_ln_g': 1.0 + nrm(ks[16], (DEPTH, MIX_W), 0.02),
        'rwkv_ln_b': nrm(ks[17], (DEPTH, MIX_W), 0.02),
        'hgrn_lb_logits': 1.0 + nrm(ks[18], (DEPTH, MIX_W), 0.1),
        'hgrn_norm_g': 1.0 + nrm(ks[19], (DEPTH, HGRN_HEAD_DIM), 0.02),
        'na_rpb': nrm(ks[20], (DEPTH, NA_HEADS, 2 * NA_KH - 1, 2 * NA_KW - 1), 0.1),
        'w_branch': nrm(ks[21], (DEPTH, N_BRANCH, MIX_W, D_MODEL), MIX_W ** -0.5),
        'w_out': nrm(ks[22], (DEPTH, D_MODEL, D_MODEL), D_MODEL ** -0.5),
        'final_g': 1.0 + nrm(ks[23], (D_MODEL,), 0.02),
    }


def reference(x, c, ctx, c_ctx, w_ada, b_ada, norm_g, w_in, rwkv_shift, rwkv_w0, rwkv_w_up, rwkv_a0, rwkv_a_up,
              rwkv_k_k, rwkv_k_a, rwkv_r_k, rwkv_ln_g, rwkv_ln_b, hgrn_lb_logits, hgrn_norm_g, na_rpb,
              w_branch, w_out, final_g):
    lbs = hgrn_lower_bounds(hgrn_lb_logits)
    xc = ctx
    for l in range(DEPTH):
        x, xc = layer(x, xc, c, c_ctx, w_ada[l], b_ada[l], norm_g[l], w_in[l], rwkv_shift[l], rwkv_w0[l],
                      rwkv_w_up[l], rwkv_a0[l], rwkv_a_up[l], rwkv_k_k[l], rwkv_k_a[l], rwkv_r_k[l],
                      rwkv_ln_g[l], rwkv_ln_b[l], lbs[l], hgrn_norm_g[l], na_rpb[l], w_branch[l], w_out[l],
                      ctx_out=(l < DEPTH - 1))
    return rmsnorm(x, final_g)
```

```python
import functools

import jax
import jax.numpy as jnp
import numpy as np
from jax import lax
from jax.experimental import pallas as pl
from jax.experimental.pallas import tpu as pltpu

F32 = jnp.float32
BF16 = jnp.bfloat16

NORM_EPS = 1e-6
RWKV_HEAD_DIM = 64
RWKV_GN_EPS = 64e-5
RWKV_CHUNK = 64
HGRN_HEAD_DIM = 128
HGRN_NORM_EPS = 1e-5
HGRN_BLOCK = 16
NA_HEAD_DIM = 128
NA_GRID_W = 64
NA_KH = 8
NA_KW = 16
NA_QROWS = 4
NA_WROWS = 12
NEG_BIG = -1e30
LANES = 128
VMEM_LIMIT = 56 * 1024 * 1024


def _cparams(n_grid):
    return pltpu.CompilerParams(dimension_semantics=("arbitrary",) * n_grid, vmem_limit_bytes=VMEM_LIMIT)


def _nn(a, b):
    return jnp.dot(a.astype(BF16), b.astype(BF16), preferred_element_type=F32)


def _nt(a, b):
    return lax.dot_general(a.astype(BF16), b.astype(BF16), (((1,), (1,)), ((), ())), preferred_element_type=F32)


def _tn(a, b):
    return lax.dot_general(a.astype(BF16), b.astype(BF16), (((0,), (0,)), ((), ())), preferred_element_type=F32)


def _nn_exact(a, b):
    return jnp.dot(a, b, preferred_element_type=F32, precision=lax.Precision.HIGHEST)


def _silu(t):
    return t * jax.nn.sigmoid(t)


def _mm_body(x_ref, w_ref, o_ref, wb_ref):
    @pl.when(pl.program_id(1) == 0)
    def _():
        wb_ref[...] = w_ref[...].astype(BF16)

    o_ref[...] = jnp.dot(x_ref[...], wb_ref[...], preferred_element_type=F32)


def _pick_tile(n, cap):
    best = None
    for t in range(LANES, cap + 1, LANES):
        if n % t == 0:
            best = t
    return best if best is not None else n


def _matmul(x, w, layer=None):
    m, k = x.shape
    n = w.shape[-1]
    tn = _pick_tile(n, 1024)
    tm = m
    for t in (1152, 1024, 768, 512, 256, 128):
        if m % t == 0:
            tm = t
            break
    if layer is None:
        w_spec = pl.BlockSpec((k, tn), lambda j, i: (0, j))
    else:
        w_spec = pl.BlockSpec((None, k, tn), lambda j, i: (layer, 0, j))
    return pl.pallas_call(
        _mm_body,
        grid=(n // tn, m // tm),
        in_specs=[pl.BlockSpec((tm, k), lambda j, i: (i, 0)), w_spec],
        out_specs=pl.BlockSpec((tm, tn), lambda j, i: (i, j)),
        out_shape=jax.ShapeDtypeStruct((m, n), F32),
        scratch_shapes=[pltpu.VMEM((k, tn), BF16)],
        compiler_params=_cparams(2),
        name="dense",
    )(x, w)


def _rwkv_body(r_ref, v_ref, kk_ref, lw0_ref, k0_ref, b0_ref, lw1_ref, k1_ref, b1_ref, o_ref, gt_ref, *, n_ctx, n_tot):
    c = RWKV_CHUNK
    hd = RWKV_HEAD_DIM
    o_ref[...] = jnp.zeros_like(o_ref)
    gt_ref[...] = jnp.zeros_like(gt_ref)

    lane = lax.broadcasted_iota(jnp.int32, (c, LANES), 1)
    row = lax.broadcasted_iota(jnp.int32, (c, LANES), 0)
    col = lane % hd
    head0 = lane < hd
    r2 = lax.broadcasted_iota(jnp.int32, (2 * hd, LANES), 0)
    l2 = lax.broadcasted_iota(jnp.int32, (2 * hd, LANES), 1)
    blockdiag = (r2 < hd) == (l2 < hd)
    tri_r = lax.broadcasted_iota(jnp.int32, (c, c), 0)
    tri_c = lax.broadcasted_iota(jnp.int32, (c, c), 1)
    eye_w = (row == col).astype(F32)

    def stack_heads(t):
        return jnp.concatenate([jnp.where(head0, t, 0.0), jnp.where(head0, 0.0, t)], axis=0)

    n_cc = n_ctx // c
    n_c = n_tot // c
    dirs = ((lw0_ref, k0_ref, b0_ref), (lw1_ref, k1_ref, b1_ref))

    def chunk(d, ci):
        lw_ref, k_ref, b_ref = dirs[d]
        rows = pl.ds(pl.multiple_of(ci * c, c), c)
        lw = lw_ref[rows, :]
        if d == 0:
            tri = (tri_r >= tri_c).astype(F32)
            strict = row > col
            incl = row >= col
        else:
            tri = (tri_r <= tri_c).astype(F32)
            strict = row < col
            incl = row <= col
        cl = _nn_exact(tri, lw)
        tot = cl[c - 1:c, :] if d == 0 else cl[0:1, :]
        e_in = jnp.exp(cl)
        e_out = jnp.exp(-cl)
        e_end = jnp.exp(tot - cl)
        r = r_ref[rows, :]
        v = v_ref[rows, :]
        kk = kk_ref[rows, :]
        k = k_ref[rows, :]
        b = b_ref[rows, :]
        kkt = kk * jnp.exp(cl - lw)
        rt = r * e_in
        kh = k * e_out
        bh = b * e_out
        x = _nt(jnp.concatenate([kkt, rt], axis=0),
                jnp.concatenate([stack_heads(kh), stack_heads(bh)], axis=0))
        a_k = jnp.where(strict, x[0:c, 0:2 * c], 0.0)
        a_b = jnp.where(strict, x[0:c, 2 * c:4 * c], 0.0)
        a_rk = jnp.where(incl, x[c:2 * c, 0:2 * c], 0.0)
        a_rb = jnp.where(incl, x[c:2 * c, 2 * c:4 * c], 0.0)

        inv = eye_w - a_b
        pw = a_b
        s = 1
        while 2 * s < c:
            pw = _nn(pw, stack_heads(pw))
            inv = inv + _nn(inv, stack_heads(pw))
            s *= 2
        gt = gt_ref[d]
        v_st = stack_heads(v)
        u = _nn(inv, stack_heads(_nt(kkt, gt) + _nn(a_k, v_st)))
        o = _nt(rt, gt) + _nn(a_rk, v_st) - _nn(a_rb, stack_heads(u))
        upd = _tn(jnp.concatenate([v, u], axis=0), jnp.concatenate([k * e_end, -(b * e_end)], axis=0))
        gt_ref[d] = gt * jnp.exp(tot) + jnp.where(blockdiag, upd, 0.0)
        o_ref[rows, :] += o

    def body(i, carry):
        chunk(0, i)
        chunk(1, jnp.where(i < n_cc, n_cc - 1 - i, n_c - 1 - (i - n_cc)))
        return carry

    lax.fori_loop(0, n_c, body, 0)


def _rwkv_scan(r, v, kk, dir_inputs, n_ctx):
    bsz, n_tot, w = r.shape
    spec = pl.BlockSpec((None, n_tot, LANES), lambda bi, hi: (bi, 0, hi))
    args = [r, v, kk]
    for lw, k, b in dir_inputs:
        args += [lw, k, b]
    return pl.pallas_call(
        functools.partial(_rwkv_body, n_ctx=n_ctx, n_tot=n_tot),
        grid=(bsz, w // LANES),
        in_specs=[spec] * len(args),
        out_specs=spec,
        out_shape=jax.ShapeDtypeStruct((bsz, n_tot, w), F32),
        scratch_shapes=[pltpu.VMEM((2, LANES, LANES), F32)],
        compiler_params=_cparams(2),
        name="rwkv_scan",
    )(*args)


def _hgrn_body(pq_ref, pf0_ref, pf1_ref, pi_ref, pg_ref, lb_ref, ng_ref, y_ref, o_ref, st_ref, *, n_ctx, n_tot):
    blk = HGRN_BLOCK
    lb = lb_ref[...]
    o_ref[...] = jnp.zeros_like(o_ref)
    st_ref[...] = jnp.zeros_like(st_ref)
    tri_r = lax.broadcasted_iota(jnp.int32, (blk, blk), 0)
    tri_c = lax.broadcasted_iota(jnp.int32, (blk, blk), 1)
    row = lax.broadcasted_iota(jnp.int32, (blk, LANES), 0)
    pf_refs = (pf0_ref, pf1_ref)
    n_cb = n_ctx // blk
    n_b = n_tot // blk

    def block(d, bi):
        rows = pl.ds(pl.multiple_of(bi * blk, blk), blk)
        q = _silu(pq_ref[rows, :])
        f = lb + (1.0 - lb) * jax.nn.sigmoid(pf_refs[d][rows, :])
        k = 1.0 - f
        g = jnp.log(f)
        v = pi_ref[rows, :]
        tri = ((tri_r >= tri_c) if d == 0 else (tri_r <= tri_c)).astype(F32)
        cl = _nn_exact(tri, g)
        tot = cl[blk - 1:blk, :] if d == 0 else cl[0:1, :]
        st = st_ref[d]
        o = _nt(q * jnp.exp(cl), st)
        for j in range(blk):
            seen = (row >= j) if d == 0 else (row <= j)
            e = jnp.exp(jnp.where(seen, cl - cl[j:j + 1, :], NEG_BIG))
            a_j = jnp.sum(q * e * k[j:j + 1, :], axis=-1, keepdims=True)
            o = o + a_j * v[j:j + 1, :]
        st_ref[d] = st * jnp.exp(tot) + _tn(v, k * jnp.exp(tot - cl))
        o_ref[rows, :] += o

    def body(i, carry):
        block(0, i)
        block(1, jnp.where(i < n_cb, n_cb - 1 - i, n_b - 1 - (i - n_cb)))
        return carry

    lax.fori_loop(0, n_b, body, 0)
    o = o_ref[...]
    on = o * lax.rsqrt(jnp.mean(o * o, axis=-1, keepdims=True) + HGRN_NORM_EPS) * ng_ref[...]
    y_ref[...] = on * _silu(pg_ref[...])


def _hgrn(p, lb, norm_g, col0, n_ctx):
    bsz, n_tot, _ = p.shape
    w = lb.shape[-1]
    nh = w // LANES
    base = col0 // LANES

    def pspec(seg):
        return pl.BlockSpec((None, n_tot, LANES), lambda bi, hi: (bi, 0, base + seg * nh + hi))

    return pl.pallas_call(
        functools.partial(_hgrn_body, n_ctx=n_ctx, n_tot=n_tot),
        grid=(bsz, nh),
        in_specs=[pspec(0), pspec(1), pspec(2), pspec(3), pspec(4),
                  pl.BlockSpec((1, LANES), lambda bi, hi: (0, hi)),
                  pl.BlockSpec((1, LANES), lambda bi, hi: (0, 0))],
        out_specs=pl.BlockSpec((None, n_tot, LANES), lambda bi, hi: (bi, 0, hi)),
        out_shape=jax.ShapeDtypeStruct((bsz, n_tot, w), F32),
        scratch_shapes=[pltpu.VMEM((n_tot, LANES), F32), pltpu.VMEM((2, LANES, LANES), F32)],
        compiler_params=_cparams(2),
        name="hgrn_scan",
    )(p, p, p, p, p, lb.reshape(1, w), norm_g.reshape(1, LANES))


def _na_tables(rows):
    gw, qr, wr = NA_GRID_W, NA_QROWS, NA_WROWS
    col = np.arange(gw)
    cs = np.clip(col - NA_KW // 2, 0, gw - NA_KW)
    kinds = ((0, 0), (qr, 0), (rows - qr, rows - wr))
    dr = np.zeros((3, qr * gw, wr * gw), np.int32)
    dc = np.zeros_like(dr)
    ok = np.zeros(dr.shape, bool)
    for t, (r0, ws) in enumerate(kinds):
        rq = r0 + np.arange(qr)[:, None, None, None]
        cq = col[None, :, None, None]
        rk = ws + np.arange(wr)[None, None, :, None]
        ck = col[None, None, None, :]
        rs = np.clip(rq - NA_KH // 2, 0, rows - NA_KH)
        csq = cs[None, :, None, None]
        valid = (rk >= rs) & (rk < rs + NA_KH) & (ck >= csq) & (ck < csq + NA_KW)
        shape = (qr, gw, wr, gw)
        ok[t] = np.broadcast_to(valid, shape).reshape(qr * gw, wr * gw)
        dr[t] = np.broadcast_to(np.clip(rk - rq + NA_KH - 1, 0, 2 * NA_KH - 2), shape).reshape(qr * gw, wr * gw)
        dc[t] = np.broadcast_to(np.clip(ck - cq + NA_KW - 1, 0, 2 * NA_KW - 2), shape).reshape(qr * gw, wr * gw)
    return dr, dc, ok


def _na_body(q_ref, k_ref, v_ref, g_ref, tab_ref, y_ref, *, n_ctx, rows, ctx_out):
    scale = NA_HEAD_DIM ** -0.5
    gw, qr, wr = NA_GRID_W, NA_QROWS, NA_WROWS
    kc = k_ref[0:n_ctx, :]
    vc = v_ref[0:n_ctx, :]
    if ctx_out:
        s = _nt(q_ref[0:n_ctx, :], kc) * scale
        p = jnp.exp(s - jnp.max(s, axis=-1, keepdims=True))
        o = _nn(p, vc) / jnp.sum(p, axis=-1, keepdims=True)
        y_ref[0:n_ctx, :] = o * _silu(g_ref[0:n_ctx, :])
    else:
        y_ref[0:n_ctx, :] = jnp.zeros((n_ctx, LANES), F32)
    n_grp = rows // qr
    for g in range(n_grp):
        r0 = g * qr
        ws = min(max(r0 - NA_KH // 2, 0), rows - wr)
        kind = 0 if g == 0 else (2 if g == n_grp - 1 else 1)
        q0 = n_ctx + r0 * gw
        k0 = n_ctx + ws * gw
        qg = q_ref[q0:q0 + qr * gw, :]
        s_loc = _nt(qg, k_ref[k0:k0 + wr * gw, :]) * scale + tab_ref[kind]
        s_ctx = _nt(qg, kc) * scale
        m = jnp.maximum(jnp.max(s_loc, axis=-1, keepdims=True), jnp.max(s_ctx, axis=-1, keepdims=True))
        p_loc = jnp.exp(s_loc - m)
        p_ctx = jnp.exp(s_ctx - m)
        den = jnp.sum(p_loc, axis=-1, keepdims=True) + jnp.sum(p_ctx, axis=-1, keepdims=True)
        o = (_nn(p_loc, v_ref[k0:k0 + wr * gw, :]) + _nn(p_ctx, vc)) / den
        y_ref[q0:q0 + qr * gw, :] = o * _silu(g_ref[q0:q0 + qr * gw, :])


def _na(p, rpb, col0, w, n_ctx, ctx_out):
    bsz, n_tot, _ = p.shape
    nh = w // NA_HEAD_DIM
    rows = (n_tot - n_ctx) // NA_GRID_W
    assert rows % NA_QROWS == 0 and rows >= NA_WROWS + NA_QROWS
    dr, dc, ok = _na_tables(rows)
    tab = jnp.where(ok[None], rpb[:, dr, dc], NEG_BIG)
    base = col0 // LANES

    def pspec(seg):
        return pl.BlockSpec((None, n_tot, LANES), lambda hi, bi: (bi, 0, base + seg * nh + hi))

    return pl.pallas_call(
        functools.partial(_na_body, n_ctx=n_ctx, rows=rows, ctx_out=ctx_out),
        grid=(nh, bsz),
        in_specs=[pspec(0), pspec(1), pspec(2), pspec(3),
                  pl.BlockSpec((None, 3, NA_QROWS * NA_GRID_W, NA_WROWS * NA_GRID_W), lambda hi, bi: (hi, 0, 0, 0))],
        out_specs=pl.BlockSpec((None, n_tot, LANES), lambda hi, bi: (bi, 0, hi)),
        out_shape=jax.ShapeDtypeStruct((bsz, n_tot, w), F32),
        compiler_params=_cparams(2),
        name="nbr_attention",
    )(p, p, p, p, tab)


def _rmsnorm(x, g, eps=NORM_EPS):
    return x * lax.rsqrt(jnp.mean(x * x, axis=-1, keepdims=True) + eps) * g


def _seg_shift(t, n_ctx, step):
    def one(s):
        z = jnp.zeros_like(s[:, :1])
        return jnp.concatenate([z, s[:, :-1]], axis=1) if step > 0 else jnp.concatenate([s[:, 1:], z], axis=1)
    return jnp.concatenate([one(t[:, :n_ctx]), one(t[:, n_ctx:])], axis=1)


def _heads(t, hd):
    return t.reshape(t.shape[:-1] + (t.shape[-1] // hd, hd))


def _pad_last(t, to):
    return jnp.pad(t, [(0, 0)] * (t.ndim - 1) + [(0, to - t.shape[-1])])


def _rwkv_branch(p, n_ctx, w, shift_w, w0, w_up, a0, a_up, k_k, k_a, r_k, ln_g, ln_b):
    bsz, n_tot, _ = p.shape
    rank = w_up.shape[1]
    n_shift = shift_w.shape[1]
    pu = p[..., :n_shift]
    gate = p[..., n_shift:n_shift + w]
    u = shift_w[0] * _seg_shift(pu, n_ctx, 1) + shift_w[1] * pu + shift_w[2] * _seg_shift(pu, n_ctx, -1)
    r, k, v = u[..., :w], u[..., w:2 * w], u[..., 2 * w:3 * w]
    lo = u[..., 3 * w:]
    rp = -(-rank // LANES) * LANES
    kk = _heads(k * k_k, RWKV_HEAD_DIM)
    kk = kk / jnp.maximum(jnp.sqrt(jnp.sum(kk * kk, axis=-1, keepdims=True)), 1e-12)
    kk = kk.reshape(bsz, n_tot, w)
    dir_inputs = []
    bonus_k = 0.0
    for j in range(2):
        wd = jnp.tanh(lo[..., j * rank:(j + 1) * rank])
        ad = lo[..., (2 + j) * rank:(3 + j) * rank]
        wl = w0[j] + _matmul(_pad_last(wd, rp).reshape(bsz * n_tot, rp).astype(BF16),
                             jnp.pad(w_up[j], ((0, rp - rank), (0, 0)))).reshape(bsz, n_tot, w)
        al = a0[j] + _matmul(_pad_last(ad, rp).reshape(bsz * n_tot, rp).astype(BF16),
                             jnp.pad(a_up[j], ((0, rp - rank), (0, 0)))).reshape(bsz, n_tot, w)
        lw = -jnp.exp(-jax.nn.softplus(-wl) - 0.5)
        a = jax.nn.sigmoid(al)
        k_j = k * (1.0 + (a - 1.0) * k_a)
        dir_inputs.append((lw, k_j, kk * a))
        bonus_k = bonus_k + k_j
    o = _rwkv_scan(r, v, kk, dir_inputs, n_ctx)
    oh = _heads(o, RWKV_HEAD_DIM)
    mu = jnp.mean(oh, axis=-1, keepdims=True)
    var = jnp.mean(jnp.square(oh - mu), axis=-1, keepdims=True)
    on = ((oh - mu) * lax.rsqrt(var + RWKV_GN_EPS)).reshape(o.shape) * ln_g + ln_b
    bonus = jnp.sum(_heads(r * bonus_k, RWKV_HEAD_DIM) * r_k, axis=-1, keepdims=True) * _heads(v, RWKV_HEAD_DIM)
    return (on + bonus.reshape(o.shape)) * _silu(gate)


def kernel(x, c, ctx, c_ctx, w_ada, b_ada, norm_g, w_in, rwkv_shift, rwkv_w0, rwkv_w_up, rwkv_a0, rwkv_a_up,
           rwkv_k_k, rwkv_k_a, rwkv_r_k, rwkv_ln_g, rwkv_ln_b, hgrn_lb_logits, hgrn_norm_g, na_rpb,
           w_branch, w_out, final_g):
    bsz, n_lat, d = x.shape
    n_ctx = ctx.shape[1]
    n_tot = n_ctx + n_lat
    depth = w_ada.shape[0]
    w = w_branch.shape[2]
    n_shift = rwkv_shift.shape[2]
    n_a = n_shift + w
    n_b = 5 * w
    n_c = 4 * w

    soft = jax.nn.softmax(hgrn_lb_logits.astype(F32), axis=0)
    lbs = jnp.cumsum(soft, axis=0) - soft[0:1]

    xx = jnp.concatenate([ctx, x], axis=1)
    cond = _silu(jnp.concatenate([c, c_ctx[None]], axis=0))
    cond = jnp.pad(cond, ((0, 16 - cond.shape[0]), (0, 0))).astype(BF16)
    is_ctx = (jnp.arange(n_tot) < n_ctx)[None, :, None]

    for l in range(depth):
        ctx_out = l < depth - 1
        ada = _matmul(cond, w_ada, layer=l)[:bsz + 1] + b_ada[l]
        shift, scale, gate = jnp.split(ada, 3, axis=-1)

        def per_token(t):
            return jnp.where(is_ctx, t[bsz][None, None, :], t[:bsz][:, None, :])

        h = _rmsnorm(xx, norm_g[l]) * (1.0 + per_token(scale)) + per_token(shift)
        p = _matmul(h.reshape(bsz * n_tot, d).astype(BF16), w_in, layer=l).reshape(bsz, n_tot, -1)
        ya = _rwkv_branch(p, n_ctx, w, rwkv_shift[l], rwkv_w0[l], rwkv_w_up[l], rwkv_a0[l], rwkv_a_up[l],
                          rwkv_k_k[l], rwkv_k_a[l], rwkv_r_k[l], rwkv_ln_g[l], rwkv_ln_b[l])
        yb = _hgrn(p, lbs[l], hgrn_norm_g[l], n_a, n_ctx)
        yc = _na(p, na_rpb[l], n_a + n_b, w, n_ctx, ctx_out)
        gates = jax.nn.sigmoid(p[..., n_a + n_b + n_c:])
        m = 0.0
        for i, y in enumerate((ya, yb, yc)):
            proj = _matmul(y.reshape(bsz * n_tot, w).astype(BF16), w_branch[l, i]).reshape(bsz, n_tot, d)
            m = m + gates[..., i * d:(i + 1) * d] * proj
        out = _matmul(m.reshape(bsz * n_tot, d).astype(BF16), w_out, layer=l).reshape(bsz, n_tot, d)
        xx = xx + per_token(gate) * out
    return _rmsnorm(xx[:, n_ctx:], final_g)
```

```python
import functools

import jax
import jax.numpy as jnp
import numpy as np
from jax import lax
from jax.experimental import pallas as pl
from jax.experimental.pallas import tpu as pltpu

F32 = jnp.float32
BF16 = jnp.bfloat16

NORM_EPS = 1e-6
RWKV_HEAD_DIM = 64
RWKV_GN_EPS = 64e-5
RWKV_CHUNK = 64
RWKV_UNROLL = 4
HGRN_HEAD_DIM = 128
HGRN_NORM_EPS = 1e-5
HGRN_BLOCK = 16
HGRN_UNROLL = 4
NA_HEAD_DIM = 128
NA_GRID_W = 64
NA_KH = 8
NA_KW = 16
NA_QROWS = 4
NA_WROWS = 12
NEG_BIG = -1e30
LANES = 128
VMEM_LIMIT = 56 * 1024 * 1024


def _cparams(n_grid):
    return pltpu.CompilerParams(dimension_semantics=("arbitrary",) * n_grid, vmem_limit_bytes=VMEM_LIMIT)


def _nn(a, b):
    return jnp.dot(a.astype(BF16), b.astype(BF16), preferred_element_type=F32)


def _nt(a, b):
    return lax.dot_general(a.astype(BF16), b.astype(BF16), (((1,), (1,)), ((), ())), preferred_element_type=F32)


def _tn(a, b):
    return lax.dot_general(a.astype(BF16), b.astype(BF16), (((0,), (0,)), ((), ())), preferred_element_type=F32)


def _nn_exact(a, b):
    return jnp.dot(a, b, preferred_element_type=F32, precision=lax.Precision.HIGHEST)


def _silu(t):
    return t * jax.nn.sigmoid(t)


def _mm_body(x_ref, w_ref, o_ref, wb_ref):
    @pl.when(pl.program_id(1) == 0)
    def _():
        wb_ref[...] = w_ref[...].astype(BF16)

    o_ref[...] = jnp.dot(x_ref[...], wb_ref[...], preferred_element_type=F32)


def _pick_tile(n, cap):
    best = None
    for t in range(LANES, cap + 1, LANES):
        if n % t == 0:
            best = t
    return best if best is not None else n


def _matmul(x, w, layer=None):
    m, k = x.shape
    n = w.shape[-1]
    tn = _pick_tile(n, 1024)
    tm = m
    for t in (1152, 1024, 768, 512, 256, 128):
        if m % t == 0:
            tm = t
            break
    if layer is None:
        w_spec = pl.BlockSpec((k, tn), lambda j, i: (0, j))
    else:
        w_spec = pl.BlockSpec((None, k, tn), lambda j, i: (layer, 0, j))
    return pl.pallas_call(
        _mm_body,
        grid=(n // tn, m // tm),
        in_specs=[pl.BlockSpec((tm, k), lambda j, i: (i, 0)), w_spec],
        out_specs=pl.BlockSpec((tm, tn), lambda j, i: (i, j)),
        out_shape=jax.ShapeDtypeStruct((m, n), F32),
        scratch_shapes=[pltpu.VMEM((k, tn), BF16)],
        compiler_params=_cparams(2),
        name="dense",
    )(x, w)


def _rwkv_body(r_ref, v_ref, kk_ref, lw0_ref, k0_ref, b0_ref, lw1_ref, k1_ref, b1_ref, o_ref, gt_ref, *, n_ctx, n_tot):
    c = RWKV_CHUNK
    hd = RWKV_HEAD_DIM
    o_ref[...] = jnp.zeros_like(o_ref)
    gt_ref[...] = jnp.zeros_like(gt_ref)

    lane = lax.broadcasted_iota(jnp.int32, (c, LANES), 1)
    row = lax.broadcasted_iota(jnp.int32, (c, LANES), 0)
    col = lane % hd
    head0 = lane < hd
    r2 = lax.broadcasted_iota(jnp.int32, (2 * hd, LANES), 0)
    l2 = lax.broadcasted_iota(jnp.int32, (2 * hd, LANES), 1)
    blockdiag = (r2 < hd) == (l2 < hd)
    tri_r = lax.broadcasted_iota(jnp.int32, (c, c), 0)
    tri_c = lax.broadcasted_iota(jnp.int32, (c, c), 1)
    eye_w = (row == col).astype(F32)

    def stack_heads(t):
        return jnp.concatenate([jnp.where(head0, t, 0.0), jnp.where(head0, 0.0, t)], axis=0)

    n_cc = n_ctx // c
    n_c = n_tot // c
    dirs = ((lw0_ref, k0_ref, b0_ref), (lw1_ref, k1_ref, b1_ref))

    def chunk(d, ci):
        lw_ref, k_ref, b_ref = dirs[d]
        rows = pl.ds(pl.multiple_of(ci * c, c), c)
        lw = lw_ref[rows, :]
        if d == 0:
            tri = (tri_r >= tri_c).astype(F32)
            strict = row > col
            incl = row >= col
        else:
            tri = (tri_r <= tri_c).astype(F32)
            strict = row < col
            incl = row <= col
        cl = _nn_exact(tri, lw)
        tot = cl[c - 1:c, :] if d == 0 else cl[0:1, :]
        e_in = jnp.exp(cl)
        e_out = jnp.exp(-cl)
        e_end = jnp.exp(tot - cl)
        r = r_ref[rows, :]
        v = v_ref[rows, :]
        kk = kk_ref[rows, :]
        k = k_ref[rows, :]
        b = b_ref[rows, :]
        kkt = kk * jnp.exp(cl - lw)
        rt = r * e_in
        kh = k * e_out
        bh = b * e_out
        x = _nt(jnp.concatenate([kkt, rt], axis=0),
                jnp.concatenate([stack_heads(kh), stack_heads(bh)], axis=0))
        a_k = jnp.where(strict, x[0:c, 0:2 * c], 0.0)
        a_b = jnp.where(strict, x[0:c, 2 * c:4 * c], 0.0)
        a_rk = jnp.where(incl, x[c:2 * c, 0:2 * c], 0.0)
        a_rb = jnp.where(incl, x[c:2 * c, 2 * c:4 * c], 0.0)

        inv = eye_w - a_b
        pw = a_b
        s = 1
        while 2 * s < c:
            pw = _nn(pw, stack_heads(pw))
            inv = inv + _nn(inv, stack_heads(pw))
            s *= 2
        gt = gt_ref[d]
        v_st = stack_heads(v)
        u = _nn(inv, stack_heads(_nt(kkt, gt) + _nn(a_k, v_st)))
        o = _nt(rt, gt) + _nn(a_rk, v_st) - _nn(a_rb, stack_heads(u))
        upd = _tn(jnp.concatenate([v, u], axis=0), jnp.concatenate([k * e_end, -(b * e_end)], axis=0))
        gt_ref[d] = gt * jnp.exp(tot) + jnp.where(blockdiag, upd, 0.0)
        o_ref[rows, :] += o

    def body(i, carry):
        chunk(0, i)
        chunk(1, jnp.where(i < n_cc, n_cc - 1 - i, n_c - 1 - (i - n_cc)))
        return carry

    lax.fori_loop(0, n_c, body, 0, unroll=RWKV_UNROLL)


def _rwkv_scan(r, v, kk, dir_inputs, n_ctx):
    bsz, n_tot, w = r.shape
    spec = pl.BlockSpec((None, n_tot, LANES), lambda bi, hi: (bi, 0, hi))
    args = [r, v, kk]
    for lw, k, b in dir_inputs:
        args += [lw, k, b]
    return pl.pallas_call(
        functools.partial(_rwkv_body, n_ctx=n_ctx, n_tot=n_tot),
        grid=(bsz, w // LANES),
        in_specs=[spec] * len(args),
        out_specs=spec,
        out_shape=jax.ShapeDtypeStruct((bsz, n_tot, w), F32),
        scratch_shapes=[pltpu.VMEM((2, LANES, LANES), F32)],
        compiler_params=_cparams(2),
        name="rwkv_scan",
    )(*args)


def _hgrn_body(pq_ref, pf0_ref, pf1_ref, pi_ref, pg_ref, lb_ref, ng_ref, y_ref, o_ref, st_ref, *, n_ctx, n_tot):
    blk = HGRN_BLOCK
    lb = lb_ref[...]
    o_ref[...] = jnp.zeros_like(o_ref)
    st_ref[...] = jnp.zeros_like(st_ref)
    tri_r = lax.broadcasted_iota(jnp.int32, (blk, blk), 0)
    tri_c = lax.broadcasted_iota(jnp.int32, (blk, blk), 1)
    row = lax.broadcasted_iota(jnp.int32, (blk, LANES), 0)
    pf_refs = (pf0_ref, pf1_ref)
    n_cb = n_ctx // blk
    n_b = n_tot // blk

    def block(d, bi):
        rows = pl.ds(pl.multiple_of(bi * blk, blk), blk)
        q = _silu(pq_ref[rows, :])
        f = lb + (1.0 - lb) * jax.nn.sigmoid(pf_refs[d][rows, :])
        k = 1.0 - f
        g = jnp.log(f)
        v = pi_ref[rows, :]
        tri = ((tri_r >= tri_c) if d == 0 else (tri_r <= tri_c)).astype(F32)
        cl = _nn_exact(tri, g)
        tot = cl[blk - 1:blk, :] if d == 0 else cl[0:1, :]
        st = st_ref[d]
        o = _nt(q * jnp.exp(cl), st)
        for j in range(blk):
            seen = (row >= j) if d == 0 else (row <= j)
            e = jnp.exp(jnp.where(seen, cl - cl[j:j + 1, :], NEG_BIG))
            a_j = jnp.sum(q * e * k[j:j + 1, :], axis=-1, keepdims=True)
            o = o + a_j * v[j:j + 1, :]
        st_ref[d] = st * jnp.exp(tot) + _tn(v, k * jnp.exp(tot - cl))
        o_ref[rows, :] += o

    def body(i, carry):
        block(0, i)
        block(1, jnp.where(i < n_cb, n_cb - 1 - i, n_b - 1 - (i - n_cb)))
        return carry

    lax.fori_loop(0, n_b, body, 0, unroll=HGRN_UNROLL)
    o = o_ref[...]
    on = o * lax.rsqrt(jnp.mean(o * o, axis=-1, keepdims=True) + HGRN_NORM_EPS) * ng_ref[...]
    y_ref[...] = on * _silu(pg_ref[...])


def _hgrn(p, lb, norm_g, col0, n_ctx):
    bsz, n_tot, _ = p.shape
    w = lb.shape[-1]
    nh = w // LANES
    base = col0 // LANES

    def pspec(seg):
        return pl.BlockSpec((None, n_tot, LANES), lambda bi, hi: (bi, 0, base + seg * nh + hi))

    return pl.pallas_call(
        functools.partial(_hgrn_body, n_ctx=n_ctx, n_tot=n_tot),
        grid=(bsz, nh),
        in_specs=[pspec(0), pspec(1), pspec(2), pspec(3), pspec(4),
                  pl.BlockSpec((1, LANES), lambda bi, hi: (0, hi)),
                  pl.BlockSpec((1, LANES), lambda bi, hi: (0, 0))],
        out_specs=pl.BlockSpec((None, n_tot, LANES), lambda bi, hi: (bi, 0, hi)),
        out_shape=jax.ShapeDtypeStruct((bsz, n_tot, w), F32),
        scratch_shapes=[pltpu.VMEM((n_tot, LANES), F32), pltpu.VMEM((2, LANES, LANES), F32)],
        compiler_params=_cparams(2),
        name="hgrn_scan",
    )(p, p, p, p, p, lb.reshape(1, w), norm_g.reshape(1, LANES))


def _na_tables(rows):
    gw, qr, wr = NA_GRID_W, NA_QROWS, NA_WROWS
    col = np.arange(gw)
    cs = np.clip(col - NA_KW // 2, 0, gw - NA_KW)
    kinds = ((0, 0), (qr, 0), (rows - qr, rows - wr))
    dc = np.clip(col[None, :] - col[:, None] + NA_KW - 1, 0, 2 * NA_KW - 2)
    col_ok = (col[None, :] >= cs[:, None]) & (col[None, :] < cs[:, None] + NA_KW)
    dr = np.zeros((3, qr, wr), np.int32)
    row_ok = np.zeros((3, qr, wr), bool)
    for t, (r0, ws) in enumerate(kinds):
        rq = r0 + np.arange(qr)[:, None]
        rk = ws + np.arange(wr)[None, :]
        rs = np.clip(rq - NA_KH // 2, 0, rows - NA_KH)
        row_ok[t] = (rk >= rs) & (rk < rs + NA_KH)
        dr[t] = np.clip(rk - rq + NA_KH - 1, 0, 2 * NA_KH - 2)
    ok = row_ok[:, :, None, :, None] & col_ok[None, None, :, None, :]
    return dr, dc, ok.reshape(3, qr * gw, wr * gw)


def _na_bias(rpb, rows):
    gw, qr, wr = NA_GRID_W, NA_QROWS, NA_WROWS
    dr, dc, ok = _na_tables(rows)
    by_col = rpb[..., dc]
    by_row = jnp.take(by_col, dr, axis=-3)
    tab = jnp.swapaxes(by_row, -3, -2).reshape(rpb.shape[:-2] + (3, qr * gw, wr * gw))
    return jnp.where(ok, tab, NEG_BIG)


def _na_body(q_ref, k_ref, v_ref, g_ref, tab_ref, y_ref, *, n_ctx, rows, ctx_out):
    scale = NA_HEAD_DIM ** -0.5
    gw, qr, wr = NA_GRID_W, NA_QROWS, NA_WROWS
    kc = k_ref[0:n_ctx, :]
    vc = v_ref[0:n_ctx, :]
    if ctx_out:
        s = _nt(q_ref[0:n_ctx, :], kc) * scale
        p = jnp.exp(s - jnp.max(s, axis=-1, keepdims=True))
        o = _nn(p, vc) / jnp.sum(p, axis=-1, keepdims=True)
        y_ref[0:n_ctx, :] = o * _silu(g_ref[0:n_ctx, :])
    else:
        y_ref[0:n_ctx, :] = jnp.zeros((n_ctx, LANES), F32)
    n_grp = rows // qr
    for g in range(n_grp):
        r0 = g * qr
        ws = min(max(r0 - NA_KH // 2, 0), rows - wr)
        kind = 0 if g == 0 else (2 if g == n_grp - 1 else 1)
        q0 = n_ctx + r0 * gw
        k0 = n_ctx + ws * gw
        qg = q_ref[q0:q0 + qr * gw, :]
        s_loc = _nt(qg, k_ref[k0:k0 + wr * gw, :]) * scale + tab_ref[kind]
        s_ctx = _nt(qg, kc) * scale
        m = jnp.maximum(jnp.max(s_loc, axis=-1, keepdims=True), jnp.max(s_ctx, axis=-1, keepdims=True))
        p_loc = jnp.exp(s_loc - m)
        p_ctx = jnp.exp(s_ctx - m)
        den = jnp.sum(p_loc, axis=-1, keepdims=True) + jnp.sum(p_ctx, axis=-1, keepdims=True)
        o = (_nn(p_loc, v_ref[k0:k0 + wr * gw, :]) + _nn(p_ctx, vc)) / den
        y_ref[q0:q0 + qr * gw, :] = o * _silu(g_ref[q0:q0 + qr * gw, :])


def _na(p, tab, col0, w, n_ctx, ctx_out):
    bsz, n_tot, _ = p.shape
    nh = w // NA_HEAD_DIM
    rows = (n_tot - n_ctx) // NA_GRID_W
    base = col0 // LANES

    def pspec(seg):
        return pl.BlockSpec((None, n_tot, LANES), lambda hi, bi: (bi, 0, base + seg * nh + hi))

    return pl.pallas_call(
        functools.partial(_na_body, n_ctx=n_ctx, rows=rows, ctx_out=ctx_out),
        grid=(nh, bsz),
        in_specs=[pspec(0), pspec(1), pspec(2), pspec(3),
                  pl.BlockSpec((None, 3, NA_QROWS * NA_GRID_W, NA_WROWS * NA_GRID_W), lambda hi, bi: (hi, 0, 0, 0))],
        out_specs=pl.BlockSpec((None, n_tot, LANES), lambda hi, bi: (bi, 0, hi)),
        out_shape=jax.ShapeDtypeStruct((bsz, n_tot, w), F32),
        compiler_params=_cparams(2),
        name="nbr_attention",
    )(p, p, p, p, tab)


def _rmsnorm(x, g, eps=NORM_EPS):
    return x * lax.rsqrt(jnp.mean(x * x, axis=-1, keepdims=True) + eps) * g


def _seg_shift(t, n_ctx, step):
    n = t.shape[1]
    pos = jnp.arange(n)[None, :, None]
    if step > 0:
        moved = jnp.pad(t, ((0, 0), (1, 0), (0, 0)))[:, :n]
        return jnp.where((pos == 0) | (pos == n_ctx), 0.0, moved)
    moved = jnp.pad(t, ((0, 0), (0, 1), (0, 0)))[:, 1:]
    return jnp.where((pos == n_ctx - 1) | (pos == n - 1), 0.0, moved)


def _heads(t, hd):
    return t.reshape(t.shape[:-1] + (t.shape[-1] // hd, hd))


def _pad_last(t, to):
    return jnp.pad(t, [(0, 0)] * (t.ndim - 1) + [(0, to - t.shape[-1])])


def _rwkv_branch(p, n_ctx, w, shift_w, w0, w_up, a0, a_up, k_k, k_a, r_k, ln_g, ln_b):
    bsz, n_tot, _ = p.shape
    rank = w_up.shape[1]
    n_shift = shift_w.shape[1]
    pu = p[..., :n_shift]
    gate = p[..., n_shift:n_shift + w]
    u = shift_w[0] * _seg_shift(pu, n_ctx, 1) + shift_w[1] * pu + shift_w[2] * _seg_shift(pu, n_ctx, -1)
    r, k, v = u[..., :w], u[..., w:2 * w], u[..., 2 * w:3 * w]
    lo = u[..., 3 * w:]
    rp = -(-rank // LANES) * LANES
    kk = _heads(k * k_k, RWKV_HEAD_DIM)
    kk = kk / jnp.maximum(jnp.sqrt(jnp.sum(kk * kk, axis=-1, keepdims=True)), 1e-12)
    kk = kk.reshape(bsz, n_tot, w)
    dir_inputs = []
    bonus_k = 0.0
    for j in range(2):
        wd = jnp.tanh(lo[..., j * rank:(j + 1) * rank])
        ad = lo[..., (2 + j) * rank:(3 + j) * rank]
        wl = w0[j] + _matmul(_pad_last(wd, rp).reshape(bsz * n_tot, rp).astype(BF16),
                             jnp.pad(w_up[j], ((0, rp - rank), (0, 0)))).reshape(bsz, n_tot, w)
        al = a0[j] + _matmul(_pad_last(ad, rp).reshape(bsz * n_tot, rp).astype(BF16),
                             jnp.pad(a_up[j], ((0, rp - rank), (0, 0)))).reshape(bsz, n_tot, w)
        lw = -jnp.exp(-jax.nn.softplus(-wl) - 0.5)
        a = jax.nn.sigmoid(al)
        k_j = k * (1.0 + (a - 1.0) * k_a)
        dir_inputs.append((lw, k_j, kk * a))
        bonus_k = bonus_k + k_j
    o = _rwkv_scan(r, v, kk, dir_inputs, n_ctx)
    oh = _heads(o, RWKV_HEAD_DIM)
    mu = jnp.mean(oh, axis=-1, keepdims=True)
    var = jnp.mean(jnp.square(oh - mu), axis=-1, keepdims=True)
    on = ((oh - mu) * lax.rsqrt(var + RWKV_GN_EPS)).reshape(o.shape) * ln_g + ln_b
    bonus = jnp.sum(_heads(r * bonus_k, RWKV_HEAD_DIM) * r_k, axis=-1, keepdims=True) * _heads(v, RWKV_HEAD_DIM)
    return (on + bonus.reshape(o.shape)) * _silu(gate)


def kernel(x, c, ctx, c_ctx, w_ada, b_ada, norm_g, w_in, rwkv_shift, rwkv_w0, rwkv_w_up, rwkv_a0, rwkv_a_up,
           rwkv_k_k, rwkv_k_a, rwkv_r_k, rwkv_ln_g, rwkv_ln_b, hgrn_lb_logits, hgrn_norm_g, na_rpb,
           w_branch, w_out, final_g):
    bsz, n_lat, d = x.shape
    n_ctx = ctx.shape[1]
    n_tot = n_ctx + n_lat
    depth = w_ada.shape[0]
    w = w_branch.shape[2]
    n_shift = rwkv_shift.shape[2]
    n_a = n_shift + w
    n_b = 5 * w
    n_c = 4 * w

    soft = jax.nn.softmax(hgrn_lb_logits.astype(F32), axis=0)
    lbs = jnp.cumsum(soft, axis=0) - soft[0:1]

    xx = jnp.concatenate([ctx, x], axis=1)
    cond = _silu(jnp.concatenate([c, c_ctx[None]], axis=0))
    cond = jnp.pad(cond, ((0, 16 - cond.shape[0]), (0, 0))).astype(BF16)
    is_ctx = (jnp.arange(n_tot) < n_ctx)[None, :, None]
    na_rows = n_lat // NA_GRID_W
    assert na_rows % NA_QROWS == 0 and na_rows >= NA_WROWS + NA_QROWS
    na_tabs = _na_bias(na_rpb, na_rows)

    for l in range(depth):
        ctx_out = l < depth - 1
        ada = _matmul(cond, w_ada, layer=l)[:bsz + 1] + b_ada[l]
        shift, scale, gate = jnp.split(ada, 3, axis=-1)

        def per_token(t):
            return jnp.where(is_ctx, t[bsz][None, None, :], t[:bsz][:, None, :])

        h = _rmsnorm(xx, norm_g[l]) * (1.0 + per_token(scale)) + per_token(shift)
        p = _matmul(h.reshape(bsz * n_tot, d).astype(BF16), w_in, layer=l).reshape(bsz, n_tot, -1)
        ya = _rwkv_branch(p, n_ctx, w, rwkv_shift[l], rwkv_w0[l], rwkv_w_up[l], rwkv_a0[l], rwkv_a_up[l],
                          rwkv_k_k[l], rwkv_k_a[l], rwkv_r_k[l], rwkv_ln_g[l], rwkv_ln_b[l])
        yb = _hgrn(p, lbs[l], hgrn_norm_g[l], n_a, n_ctx)
        yc = _na(p, na_tabs[l], n_a + n_b, w, n_ctx, ctx_out)
        gates = jax.nn.sigmoid(p[..., n_a + n_b + n_c:])
        m = 0.0
        for i, y in enumerate((ya, yb, yc)):
            proj = _matmul(y.reshape(bsz * n_tot, w).astype(BF16), w_branch[l, i]).reshape(bsz, n_tot, d)
            m = m + gates[..., i * d:(i + 1) * d] * proj
        out = _matmul(m.reshape(bsz * n_tot, d).astype(BF16), w_out, layer=l).reshape(bsz, n_tot, d)
        xx = xx + per_token(gate) * out
    return _rmsnorm(xx[:, n_ctx:], final_g)
```

```python
import functools

import jax
import jax.numpy as jnp
import numpy as np
from jax import lax
from jax.experimental import pallas as pl
from jax.experimental.pallas import tpu as pltpu

F32 = jnp.float32
BF16 = jnp.bfloat16

NORM_EPS = 1e-6
RWKV_HEAD_DIM = 64
RWKV_GN_EPS = 64e-5
RWKV_CHUNK = 64
RWKV_GROUP = 4
HGRN_HEAD_DIM = 128
HGRN_NORM_EPS = 1e-5
HGRN_BLOCK = 16
HGRN_GROUP = 4
NA_HEAD_DIM = 128
NA_GRID_W = 64
NA_KH = 8
NA_KW = 16
NA_QROWS = 4
NA_WROWS = 12
NEG_BIG = -1e30
LANES = 128
VMEM_LIMIT = 56 * 1024 * 1024


def _cparams(n_grid):
    return pltpu.CompilerParams(dimension_semantics=("arbitrary",) * n_grid, vmem_limit_bytes=VMEM_LIMIT)


def _nn(a, b):
    return jnp.dot(a.astype(BF16), b.astype(BF16), preferred_element_type=F32)


def _nt(a, b):
    return lax.dot_general(a.astype(BF16), b.astype(BF16), (((1,), (1,)), ((), ())), preferred_element_type=F32)


def _tn(a, b):
    return lax.dot_general(a.astype(BF16), b.astype(BF16), (((0,), (0,)), ((), ())), preferred_element_type=F32)


def _split3(x):
    hi = x.astype(BF16)
    r1 = x - hi.astype(F32)
    mid = r1.astype(BF16)
    return hi, mid, (r1 - mid.astype(F32)).astype(BF16)


def _mask_dot(mask, x):
    mb = mask.astype(BF16)
    hi, mid, lo = _split3(x)
    return (jnp.dot(mb, hi, preferred_element_type=F32) + jnp.dot(mb, mid, preferred_element_type=F32)
            + jnp.dot(mb, lo, preferred_element_type=F32))


def _dot_mask(x, mask):
    mb = mask.astype(BF16)
    hi, mid, lo = _split3(x)
    return (jnp.dot(hi, mb, preferred_element_type=F32) + jnp.dot(mid, mb, preferred_element_type=F32)
            + jnp.dot(lo, mb, preferred_element_type=F32))


def _silu(t):
    return t * jax.nn.sigmoid(t)


def _group(n_ctx_units, n_units, cap):
    for g in range(cap, 0, -1):
        if n_ctx_units % g == 0 and (n_units - n_ctx_units) % g == 0:
            return g
    return 1


def _scan_order(it, g, grp, n_ctx_units, n_units, reverse):
    first = it * grp
    if not reverse:
        return first + g
    return jnp.where(first < n_ctx_units, n_ctx_units - 1 - first, n_units - 1 - (first - n_ctx_units)) - g


def _mm_body(x_ref, w_ref, o_ref, wb_ref):
    @pl.when(pl.program_id(1) == 0)
    def _():
        wb_ref[...] = w_ref[...].astype(BF16)

    o_ref[...] = jnp.dot(x_ref[...], wb_ref[...], preferred_element_type=F32)


def _pick_tile(n, cap):
    best = None
    for t in range(LANES, cap + 1, LANES):
        if n % t == 0:
            best = t
    return best if best is not None else n


def _row_tile(m, cands):
    for t in cands:
        if m % t == 0:
            return t
    return m


def _matmul(x, w, layer=None):
    m, k = x.shape
    n = w.shape[-1]
    tn = _pick_tile(n, 1024)
    tm = _row_tile(m, (1152, 1024, 768, 512, 256, 128))
    if layer is None:
        w_spec = pl.BlockSpec((k, tn), lambda j, i: (0, j))
    else:
        w_spec = pl.BlockSpec((None, k, tn), lambda j, i: (layer, 0, j))
    return pl.pallas_call(
        _mm_body,
        grid=(n // tn, m // tm),
        in_specs=[pl.BlockSpec((tm, k), lambda j, i: (i, 0)), w_spec],
        out_specs=pl.BlockSpec((tm, tn), lambda j, i: (i, j)),
        out_shape=jax.ShapeDtypeStruct((m, n), F32),
        scratch_shapes=[pltpu.VMEM((k, tn), BF16)],
        compiler_params=_cparams(2),
        name="dense",
    )(x, w)


def _merge_body(*refs, n_gate):
    y_refs = refs[0:3]
    g_refs = refs[3:3 + 3 * n_gate]
    w_ref, o_ref = refs[3 + 3 * n_gate:]
    acc = None
    for br in range(3):
        gate = jnp.concatenate([g_refs[br * n_gate + q][...] for q in range(n_gate)], axis=1)
        term = jax.nn.sigmoid(gate) * jnp.dot(y_refs[br][...], w_ref[br], preferred_element_type=F32)
        acc = term if acc is None else acc + term
    o_ref[...] = acc.astype(BF16)


def _merge(ya, yb, yc, p2, gate_col0, wb, layer):
    m, w = ya.shape
    d = wb.shape[-1]
    tn = _pick_tile(d, 512)
    tm = _row_tile(m, (576, 512, 384, 256, 128))
    n_gate = tn // LANES
    y_spec = pl.BlockSpec((tm, w), lambda j, i: (i, 0))

    def g_spec(br, q):
        return pl.BlockSpec((tm, LANES), lambda j, i: (i, (gate_col0 + br * d) // LANES + j * n_gate + q))

    g_specs = [g_spec(br, q) for br in range(3) for q in range(n_gate)]
    return pl.pallas_call(
        functools.partial(_merge_body, n_gate=n_gate),
        grid=(d // tn, m // tm),
        in_specs=[y_spec, y_spec, y_spec] + g_specs + [pl.BlockSpec((None, 3, w, tn), lambda j, i: (layer, 0, 0, j))],
        out_specs=pl.BlockSpec((tm, tn), lambda j, i: (i, j)),
        out_shape=jax.ShapeDtypeStruct((m, d), BF16),
        compiler_params=_cparams(2),
        name="merge",
    )(ya, yb, yc, *([p2] * len(g_specs)), wb)


def _norm_mod(xn, ng_ref, sc_ref, sh_ref):
    y = xn * lax.rsqrt(jnp.mean(xn * xn, axis=-1, keepdims=True) + NORM_EPS) * ng_ref[...]
    return y * (1.0 + sc_ref[...]) + sh_ref[...]


def _prenorm_body(x_ref, ng_ref, sc_ref, sh_ref, h_ref):
    h_ref[...] = _norm_mod(x_ref[...], ng_ref, sc_ref, sh_ref).astype(h_ref.dtype)


def _out_body(m_ref, w_ref, x_ref, gt_ref, ng_ref, sc_ref, sh_ref, xo_ref, h_ref):
    xn = x_ref[...] + gt_ref[...] * jnp.dot(m_ref[...], w_ref[...], preferred_element_type=F32)
    xo_ref[...] = xn
    h_ref[...] = _norm_mod(xn, ng_ref, sc_ref, sh_ref).astype(h_ref.dtype)


def _token_rows(n_ctx, n_tot, bsz):
    tm = _row_tile(n_ctx, (256, 128, 64, 32, 16, 8))
    while (n_tot - n_ctx) % tm:
        tm //= 2
    per_b = n_tot // tm
    ctx_t = n_ctx // tm

    def mod_row(i):
        return jnp.where(i % per_b < ctx_t, bsz, i // per_b)

    return tm, mod_row


def _prenorm(xx2, ng, scale, shift, n_ctx, n_tot, bsz):
    m, d = xx2.shape
    tm, mod_row = _token_rows(n_ctx, n_tot, bsz)
    vec = pl.BlockSpec((None, 1, d), lambda i: (mod_row(i), 0, 0))
    return pl.pallas_call(
        _prenorm_body,
        grid=(m // tm,),
        in_specs=[pl.BlockSpec((tm, d), lambda i: (i, 0)), pl.BlockSpec((1, d), lambda i: (0, 0)), vec, vec],
        out_specs=pl.BlockSpec((tm, d), lambda i: (i, 0)),
        out_shape=jax.ShapeDtypeStruct((m, d), BF16),
        compiler_params=_cparams(1),
        name="prenorm",
    )(xx2, ng.reshape(1, d), scale[:, None, :], shift[:, None, :])


def _out_proj(m2, w_out_b, layer, xx2, gate, ng, scale, shift, n_ctx, n_tot, bsz, h_dtype):
    m, d = xx2.shape
    tm, mod_row = _token_rows(n_ctx, n_tot, bsz)
    vec = pl.BlockSpec((None, 1, d), lambda i: (mod_row(i), 0, 0))
    row = pl.BlockSpec((tm, d), lambda i: (i, 0))
    return pl.pallas_call(
        _out_body,
        grid=(m // tm,),
        in_specs=[row, pl.BlockSpec((None, d, d), lambda i: (layer, 0, 0)), row, vec,
                  pl.BlockSpec((1, d), lambda i: (0, 0)), vec, vec],
        out_specs=[row, row],
        out_shape=[jax.ShapeDtypeStruct((m, d), F32), jax.ShapeDtypeStruct((m, d), h_dtype)],
        compiler_params=_cparams(1),
        name="out_proj",
    )(m2, w_out_b, xx2, gate[:, None, :], ng.reshape(1, d), scale[:, None, :], shift[:, None, :])


def _token_shift(x, s_ref, n_ctx, n_tot):
    pos = lax.broadcasted_iota(jnp.int32, (n_tot, 1), 0)
    prev = jnp.where((pos == 0) | (pos == n_ctx), 0.0, pltpu.roll(x, 1, 0))
    nxt = jnp.where((pos == n_ctx - 1) | (pos == n_tot - 1), 0.0, pltpu.roll(x, n_tot - 1, 0))
    return s_ref[0:1, :] * prev + s_ref[1:2, :] * x + s_ref[2:3, :] * nxt


def _lora_in_body(lo_ref, s_ref, o_ref, *, n_ctx, n_tot, rank):
    lo = _token_shift(lo_ref[...], s_ref, n_ctx, n_tot)
    col = lax.broadcasted_iota(jnp.int32, lo.shape, 1)
    o_ref[...] = jnp.where(col < 2 * rank, jnp.tanh(lo), lo).astype(o_ref.dtype)


def _lora_in(p, shift_w, w, n_ctx, rank):
    bsz, n_tot, _ = p.shape
    lo_w = shift_w.shape[1] - 3 * w
    return pl.pallas_call(
        functools.partial(_lora_in_body, n_ctx=n_ctx, n_tot=n_tot, rank=rank),
        grid=(bsz,),
        in_specs=[pl.BlockSpec((None, n_tot, lo_w), lambda bi: (bi, 0, 3 * w // lo_w)),
                  pl.BlockSpec((3, lo_w), lambda bi: (0, 3 * w // lo_w))],
        out_specs=pl.BlockSpec((None, n_tot, lo_w), lambda bi: (bi, 0, 0)),
        out_shape=jax.ShapeDtypeStruct((bsz, n_tot, lo_w), BF16),
        compiler_params=_cparams(1),
        name="lora_in",
    )(p, shift_w)


def _rwkv_body(pr_ref, pk_ref, pv_ref, pg_ref, lo_ref, sr_ref, sk_ref, sv_ref, up_ref, vec_ref,
               y_ref, r_s, v_s, kk_s, lw_s, k_s, b_s, o_s, gt_ref, *, n_ctx, n_tot, grp):
    c = RWKV_CHUNK
    hd = RWKV_HEAD_DIM
    n_cc = n_ctx // c
    n_c = n_tot // c

    def shifted(p_ref, s_ref):
        return _token_shift(p_ref[...], s_ref, n_ctx, n_tot)

    lane_f = lax.broadcasted_iota(jnp.int32, (LANES, LANES), 0)
    lane_t = lax.broadcasted_iota(jnp.int32, (LANES, LANES), 1)
    same_head = ((lane_f < hd) == (lane_t < hd))
    head_sum = same_head.astype(F32)

    lo = lo_ref[...]
    r = shifted(pr_ref, sr_ref)
    k = shifted(pk_ref, sk_ref)
    r_s[...] = r
    v_s[...] = shifted(pv_ref, sv_ref)
    kk = k * vec_ref[4:5, :]
    kk = kk / jnp.maximum(jnp.sqrt(_dot_mask(kk * kk, head_sum)), 1e-12)
    kk_s[...] = kk
    k_a = vec_ref[5:6, :]
    bonus_k = None
    for j in range(2):
        wl = vec_ref[j:j + 1, :] + jnp.dot(lo, up_ref[j].astype(BF16), preferred_element_type=F32)
        al = vec_ref[2 + j:3 + j, :] + jnp.dot(lo, up_ref[2 + j].astype(BF16), preferred_element_type=F32)
        softplus = jnp.maximum(-wl, 0.0) + jnp.log(1.0 + jnp.exp(-jnp.abs(wl)))
        lw_s[j] = -jnp.exp(-softplus - 0.5)
        a = jax.nn.sigmoid(al)
        k_j = k * (1.0 + (a - 1.0) * k_a)
        k_s[j] = k_j
        b_s[j] = kk * a
        bonus_k = k_j if bonus_k is None else bonus_k + k_j
    bonus_rk = _dot_mask(r * bonus_k * vec_ref[6:7, :], head_sum)

    o_s[...] = jnp.zeros_like(o_s)
    gt_ref[...] = jnp.zeros_like(gt_ref)
    lane = lax.broadcasted_iota(jnp.int32, (c, LANES), 1)
    row = lax.broadcasted_iota(jnp.int32, (c, LANES), 0)
    col = lane % hd
    head0 = lane < hd
    lane2 = lax.broadcasted_iota(jnp.int32, (c, 2 * LANES), 1)
    head0_2 = (lane2 % LANES) < hd
    tri_r = lax.broadcasted_iota(jnp.int32, (c, c), 0)
    tri_c = lax.broadcasted_iota(jnp.int32, (c, c), 1)
    eye_w = (row == col).astype(F32)
    tri = ((tri_r >= tri_c).astype(F32), (tri_r <= tri_c).astype(F32))
    strict = (row > col, row < col)
    incl = (row >= col, row <= col)

    def stack_heads(t, h0=head0):
        return jnp.concatenate([jnp.where(h0, t, 0.0), jnp.where(h0, 0.0, t)], axis=0)

    def diag_blocks(t):
        return jnp.where(head0, t[0:c], t[c:2 * c])

    def body(it, carry):
        inst = [(d, g) for d in range(2) for g in range(grp)]
        rows = [pl.ds(pl.multiple_of(_scan_order(it, g, grp, n_cc, n_c, d == 1) * c, c), c) for d, g in inst]
        ni = len(inst)
        lw = [lw_s[d, rows[i], :] for i, (d, g) in enumerate(inst)]
        cl = [None] * ni
        for d in range(2):
            idx = [i for i in range(ni) if inst[i][0] == d]
            res = _mask_dot(tri[d], jnp.concatenate([lw[i] for i in idx], axis=1))
            for t, i in enumerate(idx):
                cl[i] = res[:, t * LANES:(t + 1) * LANES]
        tot = [cl[i][c - 1:c, :] if inst[i][0] == 0 else cl[i][0:1, :] for i in range(ni)]
        v = [v_s[rows[i], :] for i in range(ni)]
        kkt = [kk_s[rows[i], :] * jnp.exp(cl[i] - lw[i]) for i in range(ni)]
        rt = [r_s[rows[i], :] * jnp.exp(cl[i]) for i in range(ni)]
        x = []
        for i, (d, g) in enumerate(inst):
            e_out = jnp.exp(-cl[i])
            kh = k_s[d, rows[i], :] * e_out
            bh = b_s[d, rows[i], :] * e_out
            x.append(_nt(jnp.concatenate([kkt[i], rt[i]], axis=0),
                         jnp.concatenate([stack_heads(kh), stack_heads(bh)], axis=0)))
        a_k = [jnp.where(strict[inst[i][0]], x[i][0:c, 0:2 * c], 0.0) for i in range(ni)]
        a_b = [jnp.where(strict[inst[i][0]], x[i][0:c, 2 * c:4 * c], 0.0) for i in range(ni)]
        a_rk = [jnp.where(incl[inst[i][0]], x[i][c:2 * c, 0:2 * c], 0.0) for i in range(ni)]
        a_rb = [jnp.where(incl[inst[i][0]], x[i][c:2 * c, 2 * c:4 * c], 0.0) for i in range(ni)]

        inv = [eye_w - a for a in a_b]
        pw = a_b
        s = 1
        while 2 * s < c:
            pw = [_nn(t, stack_heads(t)) for t in pw]
            inv = [inv[i] + _nn(inv[i], stack_heads(pw[i])) for i in range(ni)]
            s *= 2
        v_st = [stack_heads(t) for t in v]
        akv = [_nn(a_k[i], v_st[i]) for i in range(ni)]
        t2 = [_nn(inv[i], stack_heads(jnp.concatenate([kkt[i], akv[i]], axis=1), head0_2)) for i in range(ni)]
        ro = [_nn(a_rb[i], stack_heads(t2[i], head0_2)) for i in range(ni)]
        rq = [rt[i] - ro[i][:, 0:LANES] for i in range(ni)]
        o_loc = [_nn(a_rk[i], v_st[i]) - ro[i][:, LANES:2 * LANES] for i in range(ni)]
        e_end = [jnp.exp(tot[i] - cl[i]) for i in range(ni)]
        x2 = [_tn(inv[i], b_s[inst[i][0], rows[i], :] * e_end[i]) for i in range(ni)]
        w_bd = [jnp.where(same_head, t, 0.0) for t in x2]
        w_ln = [diag_blocks(t) for t in x2]
        m_kw = [_tn(stack_heads(kkt[i]), w_bd[i]) for i in range(ni)]
        aw = [diag_blocks(_tn(a_k[i], w_ln[i])) for i in range(ni)]
        n_bd = [jnp.where(same_head, _tn(v[i], k_s[inst[i][0], rows[i], :] * e_end[i] - aw[i]), 0.0)
                for i in range(ni)]

        start = [None] * ni
        for g in range(grp):
            for d in range(2):
                i = d * grp + g
                gt = gt_ref[d]
                start[i] = gt
                gt_ref[d] = gt * jnp.exp(tot[i]) - _nn(gt, m_kw[i]) + n_bd[i]
        for i in range(ni):
            o_s[rows[i], :] += _nt(rq[i], start[i]) + o_loc[i]
        return carry

    lax.fori_loop(0, n_c // grp, body, 0)

    o = o_s[...]
    mu = _dot_mask(o, head_sum) * (1.0 / hd)
    dev = o - mu
    var = _dot_mask(dev * dev, head_sum) * (1.0 / hd)
    on = dev * lax.rsqrt(var + RWKV_GN_EPS) * vec_ref[7:8, :] + vec_ref[8:9, :]
    y_ref[...] = ((on + bonus_rk * v_s[...]) * _silu(pg_ref[...])).astype(y_ref.dtype)


def _rwkv(p, n_ctx, w, shift_w, w0, w_up, a0, a_up, k_k, k_a, r_k, ln_g, ln_b):
    bsz, n_tot, _ = p.shape
    rank = w_up.shape[1]
    n_shift = shift_w.shape[1]
    lo_w = n_shift - 3 * w
    nhp = w // LANES
    c = RWKV_CHUNK
    grp = _group(n_ctx // c, n_tot // c, RWKV_GROUP)
    up = jnp.zeros((4, lo_w, w), F32)
    for j in range(2):
        up = up.at[j, j * rank:(j + 1) * rank].set(w_up[j])
        up = up.at[2 + j, (2 + j) * rank:(3 + j) * rank].set(a_up[j])
    vec = jnp.stack([w0[0], w0[1], a0[0], a0[1], k_k, k_a, r_k.reshape(w), ln_g, ln_b], axis=0)
    vec = jnp.pad(vec, ((0, 16 - vec.shape[0]), (0, 0)))

    def pspec(seg):
        return pl.BlockSpec((None, n_tot, LANES), lambda bi, hi: (bi, 0, seg * nhp + hi))

    def sspec(seg):
        return pl.BlockSpec((3, LANES), lambda bi, hi: (0, seg * nhp + hi))

    gate_blk = n_shift // LANES
    seq = pltpu.VMEM((n_tot, LANES), F32)
    seq2 = pltpu.VMEM((2, n_tot, LANES), F32)
    return pl.pallas_call(
        functools.partial(_rwkv_body, n_ctx=n_ctx, n_tot=n_tot, grp=grp),
        grid=(bsz, nhp),
        in_specs=[pspec(0), pspec(1), pspec(2),
                  pl.BlockSpec((None, n_tot, LANES), lambda bi, hi: (bi, 0, gate_blk + hi)),
                  pl.BlockSpec((None, n_tot, lo_w), lambda bi, hi: (bi, 0, 0)),
                  sspec(0), sspec(1), sspec(2),
                  pl.BlockSpec((4, lo_w, LANES), lambda bi, hi: (0, 0, hi)),
                  pl.BlockSpec((16, LANES), lambda bi, hi: (0, hi))],
        out_specs=pl.BlockSpec((None, n_tot, LANES), lambda bi, hi: (bi, 0, hi)),
        out_shape=jax.ShapeDtypeStruct((bsz, n_tot, w), BF16),
        scratch_shapes=[seq, seq, seq, seq2, seq2, seq2, seq, pltpu.VMEM((2, LANES, LANES), F32)],
        compiler_params=_cparams(2),
        name="rwkv_mixer",
    )(p, p, p, p, _lora_in(p, shift_w, w, n_ctx, rank), shift_w, shift_w, shift_w, up, vec)


def _hgrn_body(pq_ref, pf0_ref, pf1_ref, pi_ref, pg_ref, lb_ref, ng_ref, y_ref, o_ref, st_ref, *, n_ctx, n_tot, grp):
    blk = HGRN_BLOCK
    lb = lb_ref[...]
    o_ref[...] = jnp.zeros_like(o_ref)
    st_ref[...] = jnp.zeros_like(st_ref)
    tri_r = lax.broadcasted_iota(jnp.int32, (blk, blk), 0)
    tri_c = lax.broadcasted_iota(jnp.int32, (blk, blk), 1)
    tri = ((tri_r >= tri_c).astype(F32), (tri_r <= tri_c).astype(F32))
    row = lax.broadcasted_iota(jnp.int32, (blk, LANES), 0)
    pf_refs = (pf0_ref, pf1_ref)
    n_cb = n_ctx // blk
    n_b = n_tot // blk

    def body(it, carry):
        inst = [(d, g) for d in range(2) for g in range(grp)]
        ni = len(inst)
        rows = [pl.ds(pl.multiple_of(_scan_order(it, g, grp, n_cb, n_b, d == 1) * blk, blk), blk) for d, g in inst]
        q = [_silu(pq_ref[rows[i], :]) for i in range(ni)]
        f = [lb + (1.0 - lb) * jax.nn.sigmoid(pf_refs[inst[i][0]][rows[i], :]) for i in range(ni)]
        k = [1.0 - t for t in f]
        lg = [jnp.log(t) for t in f]
        v = [pi_ref[rows[i], :] for i in range(ni)]
        cl = [None] * ni
        for d in range(2):
            idx = [i for i in range(ni) if inst[i][0] == d]
            res = _mask_dot(tri[d], jnp.concatenate([lg[i] for i in idx], axis=1))
            for t, i in enumerate(idx):
                cl[i] = res[:, t * LANES:(t + 1) * LANES]
        tot = [cl[i][blk - 1:blk, :] if inst[i][0] == 0 else cl[i][0:1, :] for i in range(ni)]
        kv = [_tn(v[i], k[i] * jnp.exp(tot[i] - cl[i])) for i in range(ni)]
        o = [None] * ni
        for j in range(blk):
            for i in range(ni):
                seen = (row >= j) if inst[i][0] == 0 else (row <= j)
                e = jnp.exp(jnp.where(seen, cl[i] - cl[i][j:j + 1, :], NEG_BIG))
                a_j = jnp.sum(q[i] * e * k[i][j:j + 1, :], axis=-1, keepdims=True)
                t = a_j * v[i][j:j + 1, :]
                o[i] = t if o[i] is None else o[i] + t
        start = [None] * ni
        for g in range(grp):
            for d in range(2):
                i = d * grp + g
                st = st_ref[d]
                start[i] = st
                st_ref[d] = st * jnp.exp(tot[i]) + kv[i]
        for i in range(ni):
            o_ref[rows[i], :] += o[i] + _nt(q[i] * jnp.exp(cl[i]), start[i])
        return carry

    lax.fori_loop(0, n_b // grp, body, 0)
    o = o_ref[...]
    on = o * lax.rsqrt(jnp.mean(o * o, axis=-1, keepdims=True) + HGRN_NORM_EPS) * ng_ref[...]
    y_ref[...] = (on * _silu(pg_ref[...])).astype(y_ref.dtype)


def _hgrn(p, lb, norm_g, col0, n_ctx):
    bsz, n_tot, _ = p.shape
    w = lb.shape[-1]
    nh = w // LANES
    base = col0 // LANES
    grp = _group(n_ctx // HGRN_BLOCK, n_tot // HGRN_BLOCK, HGRN_GROUP)

    def pspec(seg):
        return pl.BlockSpec((None, n_tot, LANES), lambda bi, hi: (bi, 0, base + seg * nh + hi))

    return pl.pallas_call(
        functools.partial(_hgrn_body, n_ctx=n_ctx, n_tot=n_tot, grp=grp),
        grid=(bsz, nh),
        in_specs=[pspec(0), pspec(1), pspec(2), pspec(3), pspec(4),
                  pl.BlockSpec((1, LANES), lambda bi, hi: (0, hi)),
                  pl.BlockSpec((1, LANES), lambda bi, hi: (0, 0))],
        out_specs=pl.BlockSpec((None, n_tot, LANES), lambda bi, hi: (bi, 0, hi)),
        out_shape=jax.ShapeDtypeStruct((bsz, n_tot, w), BF16),
        scratch_shapes=[pltpu.VMEM((n_tot, LANES), F32), pltpu.VMEM((2, LANES, LANES), F32)],
        compiler_params=_cparams(2),
        name="hgrn_mixer",
    )(p, p, p, p, p, lb.reshape(1, w), norm_g.reshape(1, LANES))


def _na_tables(rows):
    gw, qr, wr = NA_GRID_W, NA_QROWS, NA_WROWS
    col = np.arange(gw)
    cs = np.clip(col - NA_KW // 2, 0, gw - NA_KW)
    kinds = ((0, 0), (qr, 0), (rows - qr, rows - wr))
    dc = np.clip(col[None, :] - col[:, None] + NA_KW - 1, 0, 2 * NA_KW - 2)
    col_ok = (col[None, :] >= cs[:, None]) & (col[None, :] < cs[:, None] + NA_KW)
    dr = np.zeros((3, qr, wr), np.int32)
    row_ok = np.zeros((3, qr, wr), bool)
    for t, (r0, ws) in enumerate(kinds):
        rq = r0 + np.arange(qr)[:, None]
        rk = ws + np.arange(wr)[None, :]
        rs = np.clip(rq - NA_KH // 2, 0, rows - NA_KH)
        row_ok[t] = (rk >= rs) & (rk < rs + NA_KH)
        dr[t] = np.clip(rk - rq + NA_KH - 1, 0, 2 * NA_KH - 2)
    ok = row_ok[:, :, None, :, None] & col_ok[None, None, :, None, :]
    return dr, dc, ok.reshape(3, qr * gw, wr * gw)


def _na_bias(rpb, rows):
    gw, qr, wr = NA_GRID_W, NA_QROWS, NA_WROWS
    dr, dc, ok = _na_tables(rows)
    by_col = rpb[..., dc]
    by_row = jnp.take(by_col, dr, axis=-3)
    tab = jnp.swapaxes(by_row, -3, -2).reshape(rpb.shape[:-2] + (3, qr * gw, wr * gw))
    return jnp.where(ok, tab, NEG_BIG)


def _na_body(q_ref, k_ref, v_ref, g_ref, tab_ref, y_ref, *, n_ctx, rows, ctx_out):
    scale = NA_HEAD_DIM ** -0.5
    gw, qr, wr = NA_GRID_W, NA_QROWS, NA_WROWS
    kc = k_ref[0:n_ctx, :]
    vc = v_ref[0:n_ctx, :]
    if ctx_out:
        s = _nt(q_ref[0:n_ctx, :], kc) * scale
        p = jnp.exp(s - jnp.max(s, axis=-1, keepdims=True))
        o = _nn(p, vc) / jnp.sum(p, axis=-1, keepdims=True)
        y_ref[0:n_ctx, :] = (o * _silu(g_ref[0:n_ctx, :])).astype(y_ref.dtype)
    else:
        y_ref[0:n_ctx, :] = jnp.zeros((n_ctx, LANES), y_ref.dtype)
    n_grp = rows // qr
    for g in range(n_grp):
        r0 = g * qr
        ws = min(max(r0 - NA_KH // 2, 0), rows - wr)
        kind = 0 if g == 0 else (2 if g == n_grp - 1 else 1)
        q0 = n_ctx + r0 * gw
        k0 = n_ctx + ws * gw
        qg = q_ref[q0:q0 + qr * gw, :]
        s_loc = _nt(qg, k_ref[k0:k0 + wr * gw, :]) * scale + tab_ref[kind]
        s_ctx = _nt(qg, kc) * scale
        m = jnp.maximum(jnp.max(s_loc, axis=-1, keepdims=True), jnp.max(s_ctx, axis=-1, keepdims=True))
        p_loc = jnp.exp(s_loc - m)
        p_ctx = jnp.exp(s_ctx - m)
        den = jnp.sum(p_loc, axis=-1, keepdims=True) + jnp.sum(p_ctx, axis=-1, keepdims=True)
        o = (_nn(p_loc, v_ref[k0:k0 + wr * gw, :]) + _nn(p_ctx, vc)) / den
        y_ref[q0:q0 + qr * gw, :] = (o * _silu(g_ref[q0:q0 + qr * gw, :])).astype(y_ref.dtype)


def _na(p, tab, col0, w, n_ctx, ctx_out):
    bsz, n_tot, _ = p.shape
    nh = w // NA_HEAD_DIM
    rows = (n_tot - n_ctx) // NA_GRID_W
    base = col0 // LANES

    def pspec(seg):
        return pl.BlockSpec((None, n_tot, LANES), lambda hi, bi: (bi, 0, base + seg * nh + hi))

    return pl.pallas_call(
        functools.partial(_na_body, n_ctx=n_ctx, rows=rows, ctx_out=ctx_out),
        grid=(nh, bsz),
        in_specs=[pspec(0), pspec(1), pspec(2), pspec(3),
                  pl.BlockSpec((None, 3, NA_QROWS * NA_GRID_W, NA_WROWS * NA_GRID_W), lambda hi, bi: (hi, 0, 0, 0))],
        out_specs=pl.BlockSpec((None, n_tot, LANES), lambda hi, bi: (bi, 0, hi)),
        out_shape=jax.ShapeDtypeStruct((bsz, n_tot, w), BF16),
        compiler_params=_cparams(2),
        name="nbr_attention",
    )(p, p, p, p, tab)


def kernel(x, c, ctx, c_ctx, w_ada, b_ada, norm_g, w_in, rwkv_shift, rwkv_w0, rwkv_w_up, rwkv_a0, rwkv_a_up,
           rwkv_k_k, rwkv_k_a, rwkv_r_k, rwkv_ln_g, rwkv_ln_b, hgrn_lb_logits, hgrn_norm_g, na_rpb,
           w_branch, w_out, final_g):
    bsz, n_lat, d = x.shape
    n_ctx = ctx.shape[1]
    n_tot = n_ctx + n_lat
    depth = w_ada.shape[0]
    w = w_branch.shape[2]
    n_shift = rwkv_shift.shape[2]
    n_a = n_shift + w
    n_b = 5 * w
    n_c = 4 * w
    m = bsz * n_tot

    soft = jax.nn.softmax(hgrn_lb_logits.astype(F32), axis=0)
    lbs = jnp.cumsum(soft, axis=0) - soft[0:1]
    na_rows = n_lat // NA_GRID_W
    assert na_rows % NA_QROWS == 0 and na_rows >= NA_WROWS + NA_QROWS
    na_tabs = _na_bias(na_rpb, na_rows)
    wb_b = w_branch.astype(BF16)
    wo_b = w_out.astype(BF16)

    cond = _silu(jnp.concatenate([c, c_ctx[None]], axis=0))
    cond = jnp.pad(cond, ((0, 16 - cond.shape[0]), (0, 0))).astype(BF16)
    ada = [jnp.split(_matmul(cond, w_ada, layer=l)[:bsz + 1] + b_ada[l], 3, axis=-1) for l in range(depth)]
    zero_mod = jnp.zeros((bsz + 1, d), F32)

    xx = jnp.concatenate([ctx, x], axis=1).reshape(m, d)
    h = _prenorm(xx, norm_g[0], ada[0][1], ada[0][0], n_ctx, n_tot, bsz)
    for l in range(depth):
        last = l == depth - 1
        p2 = _matmul(h, w_in, layer=l)
        p = p2.reshape(bsz, n_tot, -1)
        ya = _rwkv(p, n_ctx, w, rwkv_shift[l], rwkv_w0[l], rwkv_w_up[l], rwkv_a0[l], rwkv_a_up[l],
                   rwkv_k_k[l], rwkv_k_a[l], rwkv_r_k[l], rwkv_ln_g[l], rwkv_ln_b[l])
        yb = _hgrn(p, lbs[l], hgrn_norm_g[l], n_a, n_ctx)
        yc = _na(p, na_tabs[l], n_a + n_b, w, n_ctx, not last)
        mg = _merge(ya.reshape(m, w), yb.reshape(m, w), yc.reshape(m, w), p2, n_a + n_b + n_c, wb_b, l)
        if last:
            xx, h = _out_proj(mg, wo_b, l, xx, ada[l][2], final_g, zero_mod, zero_mod, n_ctx, n_tot, bsz, F32)
        else:
            xx, h = _out_proj(mg, wo_b, l, xx, ada[l][2], norm_g[l + 1], ada[l + 1][1], ada[l + 1][0],
                              n_ctx, n_tot, bsz, BF16)
    return h.reshape(bsz, n_tot, d)[:, n_ctx:]
```

```python
import functools

import jax
import jax.numpy as jnp
import numpy as np
from jax import lax
from jax.experimental import pallas as pl
from jax.experimental.pallas import tpu as pltpu

F32 = jnp.float32
BF16 = jnp.bfloat16

NORM_EPS = 1e-6
RWKV_HEAD_DIM = 64
RWKV_GN_EPS = 64e-5
RWKV_CHUNK = 64
RWKV_GROUP_CTX = 4
RWKV_GROUP_LAT = 8
HGRN_HEAD_DIM = 128
HGRN_NORM_EPS = 1e-5
HGRN_BLOCK = 16
HGRN_GROUP = 8
NA_HEAD_DIM = 128
NA_GRID_W = 64
NA_KH = 8
NA_KW = 16
NA_QROWS = 4
NA_WROWS = 12
NEG_BIG = -1e30
LANES = 128
VMEM_LIMIT = 56 * 1024 * 1024


def _cparams(n_grid):
    return pltpu.CompilerParams(dimension_semantics=("arbitrary",) * n_grid, vmem_limit_bytes=VMEM_LIMIT)


def _nn(a, b):
    return jnp.dot(a.astype(BF16), b.astype(BF16), preferred_element_type=F32)


def _nt(a, b):
    return lax.dot_general(a.astype(BF16), b.astype(BF16), (((1,), (1,)), ((), ())), preferred_element_type=F32)


def _tn(a, b):
    return lax.dot_general(a.astype(BF16), b.astype(BF16), (((0,), (0,)), ((), ())), preferred_element_type=F32)


def _split3(x):
    hi = x.astype(BF16)
    r1 = x - hi.astype(F32)
    mid = r1.astype(BF16)
    return hi, mid, (r1 - mid.astype(F32)).astype(BF16)


def _mask_dot(mask, x):
    mb = mask.astype(BF16)
    hi, mid, lo = _split3(x)
    return (jnp.dot(mb, hi, preferred_element_type=F32) + jnp.dot(mb, mid, preferred_element_type=F32)
            + jnp.dot(mb, lo, preferred_element_type=F32))


def _dot_mask(x, mask):
    mb = mask.astype(BF16)
    hi = x.astype(BF16)
    mid = (x - hi.astype(F32)).astype(BF16)
    return jnp.dot(hi, mb, preferred_element_type=F32) + jnp.dot(mid, mb, preferred_element_type=F32)


def _silu(t):
    return t * jax.nn.sigmoid(t)


def _group(n_ctx_units, n_units, cap):
    for g in range(cap, 0, -1):
        if n_ctx_units % g == 0 and (n_units - n_ctx_units) % g == 0:
            return g
    return 1


def _scan_order(it, g, grp, n_ctx_units, n_units, reverse):
    first = it * grp
    if not reverse:
        return first + g
    return jnp.where(first < n_ctx_units, n_ctx_units - 1 - first, n_units - 1 - (first - n_ctx_units)) - g


def _mm_body(x_ref, w_ref, o_ref, wb_ref):
    @pl.when(pl.program_id(1) == 0)
    def _():
        wb_ref[...] = w_ref[...].astype(BF16)

    o_ref[...] = jnp.dot(x_ref[...], wb_ref[...], preferred_element_type=F32)


def _pick_tile(n, cap):
    best = None
    for t in range(LANES, cap + 1, LANES):
        if n % t == 0:
            best = t
    return best if best is not None else n


def _row_tile(m, cands):
    for t in cands:
        if m % t == 0:
            return t
    return m


def _matmul(x, w, layer=None):
    m, k = x.shape
    n = w.shape[-1]
    tn = min(n, 1024)
    tm = _row_tile(m, (1152, 1024, 768, 512, 256, 128))
    if layer is None:
        w_spec = pl.BlockSpec((k, tn), lambda j, i: (0, j))
    else:
        w_spec = pl.BlockSpec((None, k, tn), lambda j, i: (layer, 0, j))
    return pl.pallas_call(
        _mm_body,
        grid=(pl.cdiv(n, tn), m // tm),
        in_specs=[pl.BlockSpec((tm, k), lambda j, i: (i, 0)), w_spec],
        out_specs=pl.BlockSpec((tm, tn), lambda j, i: (i, j)),
        out_shape=jax.ShapeDtypeStruct((m, n), F32),
        scratch_shapes=[pltpu.VMEM((k, tn), BF16)],
        compiler_params=_cparams(2),
        name="dense",
    )(x, w)


def _merge_body(*refs, n_gate):
    y_refs = refs[0:3]
    g_refs = refs[3:3 + 3 * n_gate]
    w_ref, o_ref = refs[3 + 3 * n_gate:]
    acc = None
    for br in range(3):
        gate = jnp.concatenate([g_refs[br * n_gate + q][...] for q in range(n_gate)], axis=1)
        term = jax.nn.sigmoid(gate) * jnp.dot(y_refs[br][...], w_ref[br], preferred_element_type=F32)
        acc = term if acc is None else acc + term
    o_ref[...] = acc.astype(BF16)


def _merge(ya, yb, yc, p2, gate_col0, wb, layer):
    m, w = ya.shape
    d = wb.shape[-1]
    tn = _pick_tile(d, 512)
    tm = _row_tile(m, (576, 512, 384, 256, 128))
    n_gate = tn // LANES
    y_spec = pl.BlockSpec((tm, w), lambda j, i: (i, 0))

    def g_spec(br, q):
        return pl.BlockSpec((tm, LANES), lambda j, i: (i, (gate_col0 + br * d) // LANES + j * n_gate + q))

    g_specs = [g_spec(br, q) for br in range(3) for q in range(n_gate)]
    return pl.pallas_call(
        functools.partial(_merge_body, n_gate=n_gate),
        grid=(d // tn, m // tm),
        in_specs=[y_spec, y_spec, y_spec] + g_specs + [pl.BlockSpec((None, 3, w, tn), lambda j, i: (layer, 0, 0, j))],
        out_specs=pl.BlockSpec((tm, tn), lambda j, i: (i, j)),
        out_shape=jax.ShapeDtypeStruct((m, d), BF16),
        compiler_params=_cparams(2),
        name="merge",
    )(ya, yb, yc, *([p2] * len(g_specs)), wb)


def _norm_mod(xn, ng_ref, sc_ref, sh_ref):
    y = xn * lax.rsqrt(jnp.mean(xn * xn, axis=-1, keepdims=True) + NORM_EPS) * ng_ref[...]
    return y * (1.0 + sc_ref[...]) + sh_ref[...]


def _prenorm_body(x_ref, ng_ref, sc_ref, sh_ref, h_ref):
    h_ref[...] = _norm_mod(x_ref[...], ng_ref, sc_ref, sh_ref).astype(h_ref.dtype)


def _out_body(m_ref, w_ref, x_ref, gt_ref, ng_ref, sc_ref, sh_ref, xo_ref, h_ref):
    xn = x_ref[...] + gt_ref[...] * jnp.dot(m_ref[...], w_ref[...], preferred_element_type=F32)
    xo_ref[...] = xn
    h_ref[...] = _norm_mod(xn, ng_ref, sc_ref, sh_ref).astype(h_ref.dtype)


def _token_rows(n_ctx, n_tot, bsz):
    tm = _row_tile(n_ctx, (256, 128, 64, 32, 16, 8))
    while (n_tot - n_ctx) % tm:
        tm //= 2
    per_b = n_tot // tm
    ctx_t = n_ctx // tm

    def mod_row(i):
        return jnp.where(i % per_b < ctx_t, bsz, i // per_b)

    return tm, mod_row


def _prenorm(xx2, ng, scale, shift, n_ctx, n_tot, bsz):
    m, d = xx2.shape
    tm, mod_row = _token_rows(n_ctx, n_tot, bsz)
    vec = pl.BlockSpec((None, 1, d), lambda i: (mod_row(i), 0, 0))
    return pl.pallas_call(
        _prenorm_body,
        grid=(m // tm,),
        in_specs=[pl.BlockSpec((tm, d), lambda i: (i, 0)), pl.BlockSpec((1, d), lambda i: (0, 0)), vec, vec],
        out_specs=pl.BlockSpec((tm, d), lambda i: (i, 0)),
        out_shape=jax.ShapeDtypeStruct((m, d), BF16),
        compiler_params=_cparams(1),
        name="prenorm",
    )(xx2, ng.reshape(1, d), scale[:, None, :], shift[:, None, :])


def _out_proj(m2, w_out_b, layer, xx2, gate, ng, scale, shift, n_ctx, n_tot, bsz, h_dtype):
    m, d = xx2.shape
    tm, mod_row = _token_rows(n_ctx, n_tot, bsz)
    vec = pl.BlockSpec((None, 1, d), lambda i: (mod_row(i), 0, 0))
    row = pl.BlockSpec((tm, d), lambda i: (i, 0))
    return pl.pallas_call(
        _out_body,
        grid=(m // tm,),
        in_specs=[row, pl.BlockSpec((None, d, d), lambda i: (layer, 0, 0)), row, vec,
                  pl.BlockSpec((1, d), lambda i: (0, 0)), vec, vec],
        out_specs=[row, row],
        out_shape=[jax.ShapeDtypeStruct((m, d), F32), jax.ShapeDtypeStruct((m, d), h_dtype)],
        compiler_params=_cparams(1),
        name="out_proj",
    )(m2, w_out_b, xx2, gate[:, None, :], ng.reshape(1, d), scale[:, None, :], shift[:, None, :])


def _token_shift(x, s_ref, n_ctx, n_tot):
    pos = lax.broadcasted_iota(jnp.int32, (n_tot, 1), 0)
    prev = jnp.where((pos == 0) | (pos == n_ctx), 0.0, pltpu.roll(x, 1, 0))
    nxt = jnp.where((pos == n_ctx - 1) | (pos == n_tot - 1), 0.0, pltpu.roll(x, n_tot - 1, 0))
    return s_ref[0:1, :] * prev + s_ref[1:2, :] * x + s_ref[2:3, :] * nxt


def _lora_in_body(lo_ref, s_ref, o_ref, *, n_ctx, n_tot, rank):
    lo = _token_shift(lo_ref[...], s_ref, n_ctx, n_tot)
    col = lax.broadcasted_iota(jnp.int32, lo.shape, 1)
    o_ref[...] = jnp.where(col < 2 * rank, jnp.tanh(lo), lo).astype(o_ref.dtype)


def _lora_in(p, shift_w, w, n_ctx, rank):
    bsz, n_tot, _ = p.shape
    lo_w = shift_w.shape[1] - 3 * w
    return pl.pallas_call(
        functools.partial(_lora_in_body, n_ctx=n_ctx, n_tot=n_tot, rank=rank),
        grid=(bsz,),
        in_specs=[pl.BlockSpec((None, n_tot, lo_w), lambda bi: (bi, 0, 3 * w // lo_w)),
                  pl.BlockSpec((3, lo_w), lambda bi: (0, 3 * w // lo_w))],
        out_specs=pl.BlockSpec((None, n_tot, lo_w), lambda bi: (bi, 0, 0)),
        out_shape=jax.ShapeDtypeStruct((bsz, n_tot, lo_w), BF16),
        compiler_params=_cparams(1),
        name="lora_in",
    )(p, shift_w)


def _rwkv_body(pr_ref, pk_ref, pv_ref, pg_ref, lo_ref, sr_ref, sk_ref, sv_ref, up_ref, vec_ref,
               y_ref, r_s, v_s, kk_s, lw_s, k_s, b_s, o_s, gt_ref, *, n_ctx, n_tot, grp_ctx, grp_lat):
    c = RWKV_CHUNK
    hd = RWKV_HEAD_DIM
    n_cc = n_ctx // c
    n_c = n_tot // c

    def shifted(p_ref, s_ref):
        return _token_shift(p_ref[...], s_ref, n_ctx, n_tot)

    lane_f = lax.broadcasted_iota(jnp.int32, (LANES, LANES), 0)
    lane_t = lax.broadcasted_iota(jnp.int32, (LANES, LANES), 1)
    same_head = ((lane_f < hd) == (lane_t < hd))
    head_sum = same_head.astype(F32)

    lo = lo_ref[...]
    r = shifted(pr_ref, sr_ref)
    k = shifted(pk_ref, sk_ref)
    r_s[...] = r
    v_s[...] = shifted(pv_ref, sv_ref)
    kk = k * vec_ref[4:5, :]
    kk = kk / jnp.maximum(jnp.sqrt(_dot_mask(kk * kk, head_sum)), 1e-12)
    kk_s[...] = kk
    k_a = vec_ref[5:6, :]
    bonus_k = None
    for j in range(2):
        wl = vec_ref[j:j + 1, :] + jnp.dot(lo, up_ref[j].astype(BF16), preferred_element_type=F32)
        al = vec_ref[2 + j:3 + j, :] + jnp.dot(lo, up_ref[2 + j].astype(BF16), preferred_element_type=F32)
        softplus = jnp.maximum(-wl, 0.0) + jnp.log(1.0 + jnp.exp(-jnp.abs(wl)))
        lw_s[j] = -jnp.exp(-softplus - 0.5)
        a = jax.nn.sigmoid(al)
        k_j = k * (1.0 + (a - 1.0) * k_a)
        k_s[j] = k_j
        b_s[j] = kk * a
        bonus_k = k_j if bonus_k is None else bonus_k + k_j
    bonus_rk = _dot_mask(r * bonus_k * vec_ref[6:7, :], head_sum)

    o_s[...] = jnp.zeros_like(o_s)
    gt_ref[...] = jnp.zeros_like(gt_ref)
    lane = lax.broadcasted_iota(jnp.int32, (c, LANES), 1)
    row = lax.broadcasted_iota(jnp.int32, (c, LANES), 0)
    col = lane % hd
    head0 = lane < hd
    lane2 = lax.broadcasted_iota(jnp.int32, (c, 2 * LANES), 1)
    head0_2 = (lane2 % LANES) < hd
    tri_r = lax.broadcasted_iota(jnp.int32, (c, c), 0)
    tri_c = lax.broadcasted_iota(jnp.int32, (c, c), 1)
    eye_w = (row == col).astype(F32)
    tri = ((tri_r >= tri_c).astype(F32), (tri_r <= tri_c).astype(F32))
    strict = (row > col, row < col)
    incl = (row >= col, row <= col)

    def stack_heads(t, h0=head0):
        return jnp.concatenate([jnp.where(h0, t, 0.0), jnp.where(h0, 0.0, t)], axis=0)

    def diag_blocks(t):
        return jnp.where(head0, t[0:c], t[c:2 * c])

    def advance(first, last, grp):
        inst = [(d, g) for d in range(2) for g in range(grp)]
        rows = [pl.ds(pl.multiple_of((last - 1 - g if d == 1 else first + g) * c, c), c) for d, g in inst]
        ni = len(inst)
        lw = [lw_s[d, rows[i], :] for i, (d, g) in enumerate(inst)]
        cl = [None] * ni
        for d in range(2):
            idx = [i for i in range(ni) if inst[i][0] == d]
            res = _mask_dot(tri[d], jnp.concatenate([lw[i] for i in idx], axis=1))
            for t, i in enumerate(idx):
                cl[i] = res[:, t * LANES:(t + 1) * LANES]
        tot = [cl[i][c - 1:c, :] if inst[i][0] == 0 else cl[i][0:1, :] for i in range(ni)]
        v = [v_s[rows[i], :] for i in range(ni)]
        kkt = [kk_s[rows[i], :] * jnp.exp(cl[i] - lw[i]) for i in range(ni)]
        rt = [r_s[rows[i], :] * jnp.exp(cl[i]) for i in range(ni)]
        x = []
        for i, (d, g) in enumerate(inst):
            e_out = jnp.exp(-cl[i])
            kh = k_s[d, rows[i], :] * e_out
            bh = b_s[d, rows[i], :] * e_out
            x.append(_nt(jnp.concatenate([kkt[i], rt[i]], axis=0),
                         jnp.concatenate([stack_heads(kh), stack_heads(bh)], axis=0)))
        a_k = [jnp.where(strict[inst[i][0]], x[i][0:c, 0:2 * c], 0.0) for i in range(ni)]
        a_b = [jnp.where(strict[inst[i][0]], x[i][0:c, 2 * c:4 * c], 0.0) for i in range(ni)]
        a_rk = [jnp.where(incl[inst[i][0]], x[i][c:2 * c, 0:2 * c], 0.0) for i in range(ni)]
        a_rb = [jnp.where(incl[inst[i][0]], x[i][c:2 * c, 2 * c:4 * c], 0.0) for i in range(ni)]

        inv = [eye_w - a for a in a_b]
        pw = a_b
        s = 1
        while 2 * s < c:
            pw = [_nn(t, stack_heads(t)) for t in pw]
            inv = [inv[i] + _nn(inv[i], stack_heads(pw[i])) for i in range(ni)]
            s *= 2
        v_st = [stack_heads(t) for t in v]
        akv = [_nn(a_k[i], v_st[i]) for i in range(ni)]
        t2 = [_nn(inv[i], stack_heads(jnp.concatenate([kkt[i], akv[i]], axis=1), head0_2)) for i in range(ni)]
        ro = [_nn(a_rb[i], stack_heads(t2[i], head0_2)) for i in range(ni)]
        rq = [rt[i] - ro[i][:, 0:LANES] for i in range(ni)]
        o_loc = [_nn(a_rk[i], v_st[i]) - ro[i][:, LANES:2 * LANES] for i in range(ni)]
        e_end = [jnp.exp(tot[i] - cl[i]) for i in range(ni)]
        x2 = [_tn(inv[i], b_s[inst[i][0], rows[i], :] * e_end[i]) for i in range(ni)]
        w_bd = [jnp.where(same_head, t, 0.0) for t in x2]
        w_ln = [diag_blocks(t) for t in x2]
        m_kw = [_tn(stack_heads(kkt[i]), w_bd[i]) for i in range(ni)]
        aw = [diag_blocks(_tn(a_k[i], w_ln[i])) for i in range(ni)]
        n_bd = [jnp.where(same_head, _tn(v[i], k_s[inst[i][0], rows[i], :] * e_end[i] - aw[i]), 0.0)
                for i in range(ni)]

        start = [None] * ni
        for g in range(grp):
            for d in range(2):
                i = d * grp + g
                gt = gt_ref[d]
                start[i] = gt
                gt_ref[d] = gt * jnp.exp(tot[i]) - _nn(gt, m_kw[i]) + n_bd[i]
        for i in range(ni):
            o_s[rows[i], :] += _nt(rq[i], start[i]) + o_loc[i]

    for seg_first, seg_last, grp in ((0, n_cc, grp_ctx), (n_cc, n_c, grp_lat)):
        def body(it, carry, seg_first=seg_first, seg_last=seg_last, grp=grp):
            advance(seg_first + it * grp, seg_last - it * grp, grp)
            return carry

        lax.fori_loop(0, (seg_last - seg_first) // grp, body, 0)

    o = o_s[...]
    mu = _dot_mask(o, head_sum) * (1.0 / hd)
    dev = o - mu
    var = _dot_mask(dev * dev, head_sum) * (1.0 / hd)
    on = dev * lax.rsqrt(var + RWKV_GN_EPS) * vec_ref[7:8, :] + vec_ref[8:9, :]
    y_ref[...] = ((on + bonus_rk * v_s[...]) * _silu(pg_ref[...])).astype(y_ref.dtype)


def _rwkv(p, n_ctx, w, shift_w, w0, w_up, a0, a_up, k_k, k_a, r_k, ln_g, ln_b):
    bsz, n_tot, _ = p.shape
    rank = w_up.shape[1]
    n_shift = shift_w.shape[1]
    lo_w = n_shift - 3 * w
    nhp = w // LANES
    c = RWKV_CHUNK
    grp_ctx = _group(n_ctx // c, n_ctx // c, RWKV_GROUP_CTX)
    grp_lat = _group((n_tot - n_ctx) // c, (n_tot - n_ctx) // c, RWKV_GROUP_LAT)
    up = jnp.zeros((4, lo_w, w), F32)
    for j in range(2):
        up = up.at[j, j * rank:(j + 1) * rank].set(w_up[j])
        up = up.at[2 + j, (2 + j) * rank:(3 + j) * rank].set(a_up[j])
    vec = jnp.stack([w0[0], w0[1], a0[0], a0[1], k_k, k_a, r_k.reshape(w), ln_g, ln_b], axis=0)
    vec = jnp.pad(vec, ((0, 16 - vec.shape[0]), (0, 0)))

    def pspec(seg):
        return pl.BlockSpec((None, n_tot, LANES), lambda bi, hi: (bi, 0, seg * nhp + hi))

    def sspec(seg):
        return pl.BlockSpec((3, LANES), lambda bi, hi: (0, seg * nhp + hi))

    gate_blk = n_shift // LANES
    seq = pltpu.VMEM((n_tot, LANES), F32)
    seq2 = pltpu.VMEM((2, n_tot, LANES), F32)
    return pl.pallas_call(
        functools.partial(_rwkv_body, n_ctx=n_ctx, n_tot=n_tot, grp_ctx=grp_ctx, grp_lat=grp_lat),
        grid=(bsz, nhp),
        in_specs=[pspec(0), pspec(1), pspec(2),
                  pl.BlockSpec((None, n_tot, LANES), lambda bi, hi: (bi, 0, gate_blk + hi)),
                  pl.BlockSpec((None, n_tot, lo_w), lambda bi, hi: (bi, 0, 0)),
                  sspec(0), sspec(1), sspec(2),
                  pl.BlockSpec((4, lo_w, LANES), lambda bi, hi: (0, 0, hi)),
                  pl.BlockSpec((16, LANES), lambda bi, hi: (0, hi))],
        out_specs=pl.BlockSpec((None, n_tot, LANES), lambda bi, hi: (bi, 0, hi)),
        out_shape=jax.ShapeDtypeStruct((bsz, n_tot, w), BF16),
        scratch_shapes=[seq, seq, seq, seq2, seq2, seq2, seq, pltpu.VMEM((2, LANES, LANES), F32)],
        compiler_params=_cparams(2),
        name="rwkv_mixer",
    )(p, p, p, p, _lora_in(p, shift_w, w, n_ctx, rank), shift_w, shift_w, shift_w, up, vec)


def _hgrn_body(pq_ref, pf0_ref, pf1_ref, pi_ref, pg_ref, lb_ref, ng_ref, y_ref, o_ref, st_ref, *, n_ctx, n_tot, grp):
    blk = HGRN_BLOCK
    lb = lb_ref[...]
    o_ref[...] = jnp.zeros_like(o_ref)
    st_ref[...] = jnp.zeros_like(st_ref)
    tri_r = lax.broadcasted_iota(jnp.int32, (blk, blk), 0)
    tri_c = lax.broadcasted_iota(jnp.int32, (blk, blk), 1)
    tri = ((tri_r >= tri_c).astype(F32), (tri_r <= tri_c).astype(F32))
    half = blk // 2
    row = lax.broadcasted_iota(jnp.int32, (blk, LANES), 0)
    row_h = lax.broadcasted_iota(jnp.int32, (half, LANES), 0)
    pf_refs = (pf0_ref, pf1_ref)
    n_cb = n_ctx // blk
    n_b = n_tot // blk

    def body(it, carry):
        inst = [(d, g) for d in range(2) for g in range(grp)]
        ni = len(inst)
        rows = [pl.ds(pl.multiple_of(_scan_order(it, g, grp, n_cb, n_b, d == 1) * blk, blk), blk) for d, g in inst]
        q = [_silu(pq_ref[rows[i], :]) for i in range(ni)]
        f = [lb + (1.0 - lb) * jax.nn.sigmoid(pf_refs[inst[i][0]][rows[i], :]) for i in range(ni)]
        k = [1.0 - t for t in f]
        lg = [jnp.log(t) for t in f]
        v = [pi_ref[rows[i], :] for i in range(ni)]
        cl = [None] * ni
        for d in range(2):
            idx = [i for i in range(ni) if inst[i][0] == d]
            res = _mask_dot(tri[d], jnp.concatenate([lg[i] for i in idx], axis=1))
            for t, i in enumerate(idx):
                cl[i] = res[:, t * LANES:(t + 1) * LANES]
        tot = [cl[i][blk - 1:blk, :] if inst[i][0] == 0 else cl[i][0:1, :] for i in range(ni)]
        kv = [_tn(v[i], k[i] * jnp.exp(tot[i] - cl[i])) for i in range(ni)]
        o = [None] * ni
        for i in range(ni):
            rev = inst[i][0] == 1
            ref_row = half if rev else half - 1
            early = (row >= half) if rev else (row < half)
            beta = cl[i][ref_row:ref_row + 1, :]
            qn = q[i] * jnp.exp(jnp.where(early, NEG_BIG, cl[i] - beta))
            kn = k[i] * jnp.exp(jnp.where(early, beta - cl[i], NEG_BIG))
            o[i] = _nn(_nt(qn, kn), v[i])
        o_half = [[None, None] for _ in range(ni)]
        for j in range(blk):
            hj = j // half
            lo_r, hi_r = hj * half, (hj + 1) * half
            for i in range(ni):
                seen = (row_h >= j - lo_r) if inst[i][0] == 0 else (row_h <= j - lo_r)
                cl_h = cl[i][lo_r:hi_r, :]
                e = jnp.exp(jnp.where(seen, cl_h - cl[i][j:j + 1, :], NEG_BIG))
                a_j = jnp.sum(q[i][lo_r:hi_r, :] * e * k[i][j:j + 1, :], axis=-1, keepdims=True)
                t = a_j * v[i][j:j + 1, :]
                o_half[i][hj] = t if o_half[i][hj] is None else o_half[i][hj] + t
        o = [o[i] + jnp.concatenate(o_half[i], axis=0) for i in range(ni)]
        start = [None] * ni
        for g in range(grp):
            for d in range(2):
                i = d * grp + g
                st = st_ref[d]
                start[i] = st
                st_ref[d] = st * jnp.exp(tot[i]) + kv[i]
        for i in range(ni):
            o_ref[rows[i], :] += o[i] + _nt(q[i] * jnp.exp(cl[i]), start[i])
        return carry

    lax.fori_loop(0, n_b // grp, body, 0)
    o = o_ref[...]
    on = o * lax.rsqrt(jnp.mean(o * o, axis=-1, keepdims=True) + HGRN_NORM_EPS) * ng_ref[...]
    y_ref[...] = (on * _silu(pg_ref[...])).astype(y_ref.dtype)


def _hgrn(p, lb, norm_g, col0, n_ctx):
    bsz, n_tot, _ = p.shape
    w = lb.shape[-1]
    nh = w // LANES
    base = col0 // LANES
    grp = _group(n_ctx // HGRN_BLOCK, n_tot // HGRN_BLOCK, HGRN_GROUP)

    def pspec(seg):
        return pl.BlockSpec((None, n_tot, LANES), lambda bi, hi: (bi, 0, base + seg * nh + hi))

    return pl.pallas_call(
        functools.partial(_hgrn_body, n_ctx=n_ctx, n_tot=n_tot, grp=grp),
        grid=(bsz, nh),
        in_specs=[pspec(0), pspec(1), pspec(2), pspec(3), pspec(4),
                  pl.BlockSpec((1, LANES), lambda bi, hi: (0, hi)),
                  pl.BlockSpec((1, LANES), lambda bi, hi: (0, 0))],
        out_specs=pl.BlockSpec((None, n_tot, LANES), lambda bi, hi: (bi, 0, hi)),
        out_shape=jax.ShapeDtypeStruct((bsz, n_tot, w), BF16),
        scratch_shapes=[pltpu.VMEM((n_tot, LANES), F32), pltpu.VMEM((2, LANES, LANES), F32)],
        compiler_params=_cparams(2),
        name="hgrn_mixer",
    )(p, p, p, p, p, lb.reshape(1, w), norm_g.reshape(1, LANES))


def _na_tables(rows):
    gw, qr, wr = NA_GRID_W, NA_QROWS, NA_WROWS
    col = np.arange(gw)
    cs = np.clip(col - NA_KW // 2, 0, gw - NA_KW)
    kinds = ((0, 0), (qr, 0), (rows - qr, rows - wr))
    dc = np.clip(col[None, :] - col[:, None] + NA_KW - 1, 0, 2 * NA_KW - 2)
    col_ok = (col[None, :] >= cs[:, None]) & (col[None, :] < cs[:, None] + NA_KW)
    dr = np.zeros((3, qr, wr), np.int32)
    row_ok = np.zeros((3, qr, wr), bool)
    for t, (r0, ws) in enumerate(kinds):
        rq = r0 + np.arange(qr)[:, None]
        rk = ws + np.arange(wr)[None, :]
        rs = np.clip(rq - NA_KH // 2, 0, rows - NA_KH)
        row_ok[t] = (rk >= rs) & (rk < rs + NA_KH)
        dr[t] = np.clip(rk - rq + NA_KH - 1, 0, 2 * NA_KH - 2)
    ok = row_ok[:, :, None, :, None] & col_ok[None, None, :, None, :]
    return dr, dc, ok.reshape(3, qr * gw, wr * gw)


def _na_bias(rpb, rows):
    gw, qr, wr = NA_GRID_W, NA_QROWS, NA_WROWS
    dr, dc, ok = _na_tables(rows)
    by_col = rpb[..., dc]
    by_row = jnp.take(by_col, dr, axis=-3)
    tab = jnp.swapaxes(by_row, -3, -2).reshape(rpb.shape[:-2] + (3, qr * gw, wr * gw))
    return jnp.where(ok, tab, NEG_BIG)


def _na_body(q_ref, k_ref, v_ref, g_ref, tab_ref, y_ref, *, n_ctx, rows, ctx_out):
    scale = NA_HEAD_DIM ** -0.5
    gw, qr, wr = NA_GRID_W, NA_QROWS, NA_WROWS
    kc = k_ref[0:n_ctx, :]
    vc = v_ref[0:n_ctx, :]
    if ctx_out:
        s = _nt(q_ref[0:n_ctx, :], kc) * scale
        p = jnp.exp(s - jnp.max(s, axis=-1, keepdims=True))
        o = _nn(p, vc) / jnp.sum(p, axis=-1, keepdims=True)
        y_ref[0:n_ctx, :] = (o * _silu(g_ref[0:n_ctx, :])).astype(y_ref.dtype)
    else:
        y_ref[0:n_ctx, :] = jnp.zeros((n_ctx, LANES), y_ref.dtype)
    n_grp = rows // qr
    for g in range(n_grp):
        r0 = g * qr
        ws = min(max(r0 - NA_KH // 2, 0), rows - wr)
        kind = 0 if g == 0 else (2 if g == n_grp - 1 else 1)
        q0 = n_ctx + r0 * gw
        k0 = n_ctx + ws * gw
        qg = q_ref[q0:q0 + qr * gw, :]
        s_loc = _nt(qg, k_ref[k0:k0 + wr * gw, :]) * scale + tab_ref[kind]
        s_ctx = _nt(qg, kc) * scale
        m = jnp.maximum(jnp.max(s_loc, axis=-1, keepdims=True), jnp.max(s_ctx, axis=-1, keepdims=True))
        p_loc = jnp.exp(s_loc - m)
        p_ctx = jnp.exp(s_ctx - m)
        den = jnp.sum(p_loc, axis=-1, keepdims=True) + jnp.sum(p_ctx, axis=-1, keepdims=True)
        o = (_nn(p_loc, v_ref[k0:k0 + wr * gw, :]) + _nn(p_ctx, vc)) / den
        y_ref[q0:q0 + qr * gw, :] = (o * _silu(g_ref[q0:q0 + qr * gw, :])).astype(y_ref.dtype)


def _na(p, tab, col0, w, n_ctx, ctx_out):
    bsz, n_tot, _ = p.shape
    nh = w // NA_HEAD_DIM
    rows = (n_tot - n_ctx) // NA_GRID_W
    base = col0 // LANES

    def pspec(seg):
        return pl.BlockSpec((None, n_tot, LANES), lambda hi, bi: (bi, 0, base + seg * nh + hi))

    return pl.pallas_call(
        functools.partial(_na_body, n_ctx=n_ctx, rows=rows, ctx_out=ctx_out),
        grid=(nh, bsz),
        in_specs=[pspec(0), pspec(1), pspec(2), pspec(3),
                  pl.BlockSpec((None, 3, NA_QROWS * NA_GRID_W, NA_WROWS * NA_GRID_W), lambda hi, bi: (hi, 0, 0, 0))],
        out_specs=pl.BlockSpec((None, n_tot, LANES), lambda hi, bi: (bi, 0, hi)),
        out_shape=jax.ShapeDtypeStruct((bsz, n_tot, w), BF16),
        compiler_params=_cparams(2),
        name="nbr_attention",
    )(p, p, p, p, tab)


def kernel(x, c, ctx, c_ctx, w_ada, b_ada, norm_g, w_in, rwkv_shift, rwkv_w0, rwkv_w_up, rwkv_a0, rwkv_a_up,
           rwkv_k_k, rwkv_k_a, rwkv_r_k, rwkv_ln_g, rwkv_ln_b, hgrn_lb_logits, hgrn_norm_g, na_rpb,
           w_branch, w_out, final_g):
    bsz, n_lat, d = x.shape
    n_ctx = ctx.shape[1]
    n_tot = n_ctx + n_lat
    depth = w_ada.shape[0]
    w = w_branch.shape[2]
    n_shift = rwkv_shift.shape[2]
    n_a = n_shift + w
    n_b = 5 * w
    n_c = 4 * w
    m = bsz * n_tot

    soft = jax.nn.softmax(hgrn_lb_logits.astype(F32), axis=0)
    lbs = jnp.cumsum(soft, axis=0) - soft[0:1]
    na_rows = n_lat // NA_GRID_W
    assert na_rows % NA_QROWS == 0 and na_rows >= NA_WROWS + NA_QROWS
    na_tabs = _na_bias(na_rpb, na_rows)
    wb_b = w_branch.astype(BF16)
    wo_b = w_out.astype(BF16)

    cond = _silu(jnp.concatenate([c, c_ctx[None]], axis=0))
    cond = jnp.pad(cond, ((0, 16 - cond.shape[0]), (0, 0))).astype(BF16)
    ada = [jnp.split(_matmul(cond, w_ada, layer=l)[:bsz + 1] + b_ada[l], 3, axis=-1) for l in range(depth)]
    zero_mod = jnp.zeros((bsz + 1, d), F32)

    xx = jnp.concatenate([ctx, x], axis=1).reshape(m, d)
    h = _prenorm(xx, norm_g[0], ada[0][1], ada[0][0], n_ctx, n_tot, bsz)
    for l in range(depth):
        last = l == depth - 1
        p2 = _matmul(h, w_in, layer=l)
        p = p2.reshape(bsz, n_tot, -1)
        ya = _rwkv(p, n_ctx, w, rwkv_shift[l], rwkv_w0[l], rwkv_w_up[l], rwkv_a0[l], rwkv_a_up[l],
                   rwkv_k_k[l], rwkv_k_a[l], rwkv_r_k[l], rwkv_ln_g[l], rwkv_ln_b[l])
        yb = _hgrn(p, lbs[l], hgrn_norm_g[l], n_a, n_ctx)
        yc = _na(p, na_tabs[l], n_a + n_b, w, n_ctx, not last)
        mg = _merge(ya.reshape(m, w), yb.reshape(m, w), yc.reshape(m, w), p2, n_a + n_b + n_c, wb_b, l)
        if last:
            xx, h = _out_proj(mg, wo_b, l, xx, ada[l][2], final_g, zero_mod, zero_mod, n_ctx, n_tot, bsz, F32)
        else:
            xx, h = _out_proj(mg, wo_b, l, xx, ada[l][2], norm_g[l + 1], ada[l + 1][1], ada[l + 1][0],
                              n_ctx, n_tot, bsz, BF16)
    return h.reshape(bsz, n_tot, d)[:, n_ctx:]
```

```python
import functools

import jax
import jax.numpy as jnp
import numpy as np
from jax import lax
from jax.experimental import pallas as pl
from jax.experimental.pallas import tpu as pltpu

F32 = jnp.float32
BF16 = jnp.bfloat16

NORM_EPS = 1e-6
RWKV_HEAD_DIM = 64
RWKV_GN_EPS = 64e-5
RWKV_CHUNK = 64
RWKV_GROUP_CTX = 4
RWKV_GROUP_LAT = 8
HGRN_HEAD_DIM = 128
HGRN_NORM_EPS = 1e-5
HGRN_BLOCK = 16
HGRN_GROUP = 8
NA_HEAD_DIM = 128
NA_GRID_W = 64
NA_KH = 8
NA_KW = 16
NA_QROWS = 4
NA_WROWS = 12
NA_SLAB_PAD = 4
NA_SLABS = 24
NEG_BIG = -1e30
LANES = 128
VMEM_LIMIT = 56 * 1024 * 1024


def _cparams(n_grid):
    return pltpu.CompilerParams(dimension_semantics=("arbitrary",) * n_grid, vmem_limit_bytes=VMEM_LIMIT)


def _nn(a, b):
    return jnp.dot(a.astype(BF16), b.astype(BF16), preferred_element_type=F32)


def _nt(a, b):
    return lax.dot_general(a.astype(BF16), b.astype(BF16), (((1,), (1,)), ((), ())), preferred_element_type=F32)


def _tn(a, b):
    return lax.dot_general(a.astype(BF16), b.astype(BF16), (((0,), (0,)), ((), ())), preferred_element_type=F32)


def _split3(x):
    hi = x.astype(BF16)
    r1 = x - hi.astype(F32)
    mid = r1.astype(BF16)
    return hi, mid, (r1 - mid.astype(F32)).astype(BF16)


def _mask_dot(mask, x):
    mb = mask.astype(BF16)
    hi, mid, lo = _split3(x)
    return (jnp.dot(mb, hi, preferred_element_type=F32) + jnp.dot(mb, mid, preferred_element_type=F32)
            + jnp.dot(mb, lo, preferred_element_type=F32))


def _dot_mask(x, mask):
    mb = mask.astype(BF16)
    hi = x.astype(BF16)
    mid = (x - hi.astype(F32)).astype(BF16)
    return jnp.dot(hi, mb, preferred_element_type=F32) + jnp.dot(mid, mb, preferred_element_type=F32)


def _silu(t):
    return t * jax.nn.sigmoid(t)


def _group(n_ctx_units, n_units, cap):
    for g in range(cap, 0, -1):
        if n_ctx_units % g == 0 and (n_units - n_ctx_units) % g == 0:
            return g
    return 1


def _scan_order(it, g, grp, n_ctx_units, n_units, reverse):
    first = it * grp
    if not reverse:
        return first + g
    return jnp.where(first < n_ctx_units, n_ctx_units - 1 - first, n_units - 1 - (first - n_ctx_units)) - g


def _mm_body(x_ref, w_ref, o_ref, wb_ref):
    @pl.when(pl.program_id(1) == 0)
    def _():
        wb_ref[...] = w_ref[...].astype(BF16)

    o_ref[...] = jnp.dot(x_ref[...], wb_ref[...], preferred_element_type=F32)


def _pick_tile(n, cap):
    best = None
    for t in range(LANES, cap + 1, LANES):
        if n % t == 0:
            best = t
    return best if best is not None else n


def _row_tile(m, cands):
    for t in cands:
        if m % t == 0:
            return t
    return m


def _matmul(x, w, layer=None):
    m, k = x.shape
    n = w.shape[-1]
    tn = min(n, 1024)
    tm = _row_tile(m, (1152, 1024, 768, 512, 256, 128))
    if layer is None:
        w_spec = pl.BlockSpec((k, tn), lambda j, i: (0, j))
    else:
        w_spec = pl.BlockSpec((None, k, tn), lambda j, i: (layer, 0, j))
    return pl.pallas_call(
        _mm_body,
        grid=(pl.cdiv(n, tn), m // tm),
        in_specs=[pl.BlockSpec((tm, k), lambda j, i: (i, 0)), w_spec],
        out_specs=pl.BlockSpec((tm, tn), lambda j, i: (i, j)),
        out_shape=jax.ShapeDtypeStruct((m, n), F32),
        scratch_shapes=[pltpu.VMEM((k, tn), BF16)],
        compiler_params=_cparams(2),
        name="dense",
    )(x, w)


def _merge_body(*refs, n_gate):
    y_refs = refs[0:3]
    g_refs = refs[3:3 + 3 * n_gate]
    w_ref, o_ref = refs[3 + 3 * n_gate:]
    acc = None
    for br in range(3):
        gate = jnp.concatenate([g_refs[br * n_gate + q][...] for q in range(n_gate)], axis=1)
        term = jax.nn.sigmoid(gate) * jnp.dot(y_refs[br][...], w_ref[br], preferred_element_type=F32)
        acc = term if acc is None else acc + term
    o_ref[...] = acc.astype(BF16)


def _merge(ya, yb, yc, p2, gate_col0, wb, layer):
    m, w = ya.shape
    d = wb.shape[-1]
    tn = _pick_tile(d, 512)
    tm = _row_tile(m, (576, 512, 384, 256, 128))
    n_gate = tn // LANES
    y_spec = pl.BlockSpec((tm, w), lambda j, i: (i, 0))

    def g_spec(br, q):
        return pl.BlockSpec((tm, LANES), lambda j, i: (i, (gate_col0 + br * d) // LANES + j * n_gate + q))

    g_specs = [g_spec(br, q) for br in range(3) for q in range(n_gate)]
    return pl.pallas_call(
        functools.partial(_merge_body, n_gate=n_gate),
        grid=(d // tn, m // tm),
        in_specs=[y_spec, y_spec, y_spec] + g_specs + [pl.BlockSpec((None, 3, w, tn), lambda j, i: (layer, 0, 0, j))],
        out_specs=pl.BlockSpec((tm, tn), lambda j, i: (i, j)),
        out_shape=jax.ShapeDtypeStruct((m, d), BF16),
        compiler_params=_cparams(2),
        name="merge",
    )(ya, yb, yc, *([p2] * len(g_specs)), wb)


def _norm_mod(xn, ng_ref, sc_ref, sh_ref):
    y = xn * lax.rsqrt(jnp.mean(xn * xn, axis=-1, keepdims=True) + NORM_EPS) * ng_ref[...]
    return y * (1.0 + sc_ref[...]) + sh_ref[...]


def _prenorm_body(x_ref, ng_ref, sc_ref, sh_ref, h_ref):
    h_ref[...] = _norm_mod(x_ref[...], ng_ref, sc_ref, sh_ref).astype(h_ref.dtype)


def _out_body(m_ref, w_ref, x_ref, gt_ref, ng_ref, sc_ref, sh_ref, xo_ref, h_ref):
    xn = x_ref[...] + gt_ref[...] * jnp.dot(m_ref[...], w_ref[...], preferred_element_type=F32)
    xo_ref[...] = xn
    h_ref[...] = _norm_mod(xn, ng_ref, sc_ref, sh_ref).astype(h_ref.dtype)


def _token_rows(n_ctx, n_tot, bsz):
    tm = _row_tile(n_ctx, (256, 128, 64, 32, 16, 8))
    while (n_tot - n_ctx) % tm:
        tm //= 2
    per_b = n_tot // tm
    ctx_t = n_ctx // tm

    def mod_row(i):
        return jnp.where(i % per_b < ctx_t, bsz, i // per_b)

    return tm, mod_row


def _prenorm(xx2, ng, scale, shift, n_ctx, n_tot, bsz):
    m, d = xx2.shape
    tm, mod_row = _token_rows(n_ctx, n_tot, bsz)
    vec = pl.BlockSpec((None, 1, d), lambda i: (mod_row(i), 0, 0))
    return pl.pallas_call(
        _prenorm_body,
        grid=(m // tm,),
        in_specs=[pl.BlockSpec((tm, d), lambda i: (i, 0)), pl.BlockSpec((1, d), lambda i: (0, 0)), vec, vec],
        out_specs=pl.BlockSpec((tm, d), lambda i: (i, 0)),
        out_shape=jax.ShapeDtypeStruct((m, d), BF16),
        compiler_params=_cparams(1),
        name="prenorm",
    )(xx2, ng.reshape(1, d), scale[:, None, :], shift[:, None, :])


def _out_proj(m2, w_out_b, layer, xx2, gate, ng, scale, shift, n_ctx, n_tot, bsz, h_dtype):
    m, d = xx2.shape
    tm, mod_row = _token_rows(n_ctx, n_tot, bsz)
    vec = pl.BlockSpec((None, 1, d), lambda i: (mod_row(i), 0, 0))
    row = pl.BlockSpec((tm, d), lambda i: (i, 0))
    return pl.pallas_call(
        _out_body,
        grid=(m // tm,),
        in_specs=[row, pl.BlockSpec((None, d, d), lambda i: (layer, 0, 0)), row, vec,
                  pl.BlockSpec((1, d), lambda i: (0, 0)), vec, vec],
        out_specs=[row, row],
        out_shape=[jax.ShapeDtypeStruct((m, d), F32), jax.ShapeDtypeStruct((m, d), h_dtype)],
        compiler_params=_cparams(1),
        name="out_proj",
    )(m2, w_out_b, xx2, gate[:, None, :], ng.reshape(1, d), scale[:, None, :], shift[:, None, :])


def _token_shift(x, s_ref, n_ctx, n_tot):
    pos = lax.broadcasted_iota(jnp.int32, (n_tot, 1), 0)
    prev = jnp.where((pos == 0) | (pos == n_ctx), 0.0, pltpu.roll(x, 1, 0))
    nxt = jnp.where((pos == n_ctx - 1) | (pos == n_tot - 1), 0.0, pltpu.roll(x, n_tot - 1, 0))
    return s_ref[0:1, :] * prev + s_ref[1:2, :] * x + s_ref[2:3, :] * nxt


def _lora_in_body(lo_ref, s_ref, o_ref, *, n_ctx, n_tot, rank):
    lo = _token_shift(lo_ref[...], s_ref, n_ctx, n_tot)
    col = lax.broadcasted_iota(jnp.int32, lo.shape, 1)
    o_ref[...] = jnp.where(col < 2 * rank, jnp.tanh(lo), lo).astype(o_ref.dtype)


def _lora_in(p, shift_w, w, n_ctx, rank):
    bsz, n_tot, _ = p.shape
    lo_w = shift_w.shape[1] - 3 * w
    return pl.pallas_call(
        functools.partial(_lora_in_body, n_ctx=n_ctx, n_tot=n_tot, rank=rank),
        grid=(bsz,),
        in_specs=[pl.BlockSpec((None, n_tot, lo_w), lambda bi: (bi, 0, 3 * w // lo_w)),
                  pl.BlockSpec((3, lo_w), lambda bi: (0, 3 * w // lo_w))],
        out_specs=pl.BlockSpec((None, n_tot, lo_w), lambda bi: (bi, 0, 0)),
        out_shape=jax.ShapeDtypeStruct((bsz, n_tot, lo_w), BF16),
        compiler_params=_cparams(1),
        name="lora_in",
    )(p, shift_w)


def _rwkv_body(pr_ref, pk_ref, pv_ref, pg_ref, lo_ref, sr_ref, sk_ref, sv_ref, up_ref, vec_ref,
               y_ref, r_s, v_s, kk_s, lw_s, k_s, b_s, o_s, gt_ref, *, n_ctx, n_tot, grp_ctx, grp_lat):
    c = RWKV_CHUNK
    hd = RWKV_HEAD_DIM
    n_cc = n_ctx // c
    n_c = n_tot // c

    def shifted(p_ref, s_ref):
        return _token_shift(p_ref[...], s_ref, n_ctx, n_tot)

    lane_f = lax.broadcasted_iota(jnp.int32, (LANES, LANES), 0)
    lane_t = lax.broadcasted_iota(jnp.int32, (LANES, LANES), 1)
    same_head = ((lane_f < hd) == (lane_t < hd))
    head_sum = same_head.astype(F32)

    lo = lo_ref[...]
    r = shifted(pr_ref, sr_ref)
    k = shifted(pk_ref, sk_ref)
    r_s[...] = r
    v_s[...] = shifted(pv_ref, sv_ref)
    kk = k * vec_ref[4:5, :]
    kk = kk / jnp.maximum(jnp.sqrt(_dot_mask(kk * kk, head_sum)), 1e-12)
    kk_s[...] = kk
    k_a = vec_ref[5:6, :]
    bonus_k = None
    for j in range(2):
        wl = vec_ref[j:j + 1, :] + jnp.dot(lo, up_ref[j].astype(BF16), preferred_element_type=F32)
        al = vec_ref[2 + j:3 + j, :] + jnp.dot(lo, up_ref[2 + j].astype(BF16), preferred_element_type=F32)
        softplus = jnp.maximum(-wl, 0.0) + jnp.log(1.0 + jnp.exp(-jnp.abs(wl)))
        lw_s[j] = -jnp.exp(-softplus - 0.5)
        a = jax.nn.sigmoid(al)
        k_j = k * (1.0 + (a - 1.0) * k_a)
        k_s[j] = k_j
        b_s[j] = kk * a
        bonus_k = k_j if bonus_k is None else bonus_k + k_j
    bonus_rk = _dot_mask(r * bonus_k * vec_ref[6:7, :], head_sum)

    o_s[...] = jnp.zeros_like(o_s)
    gt_ref[...] = jnp.zeros_like(gt_ref)
    lane = lax.broadcasted_iota(jnp.int32, (c, LANES), 1)
    row = lax.broadcasted_iota(jnp.int32, (c, LANES), 0)
    col = lane % hd
    head0 = lane < hd
    lane2 = lax.broadcasted_iota(jnp.int32, (c, 2 * LANES), 1)
    head0_2 = (lane2 % LANES) < hd
    tri_r = lax.broadcasted_iota(jnp.int32, (c, c), 0)
    tri_c = lax.broadcasted_iota(jnp.int32, (c, c), 1)
    eye_w = (row == col).astype(F32)
    tri = ((tri_r >= tri_c).astype(F32), (tri_r <= tri_c).astype(F32))
    strict = (row > col, row < col)
    incl = (row >= col, row <= col)

    def stack_heads(t, h0=head0):
        return jnp.concatenate([jnp.where(h0, t, 0.0), jnp.where(h0, 0.0, t)], axis=0)

    def diag_blocks(t):
        return jnp.where(head0, t[0:c], t[c:2 * c])

    def advance(first, last, grp):
        inst = [(d, g) for d in range(2) for g in range(grp)]
        rows = [pl.ds(pl.multiple_of((last - 1 - g if d == 1 else first + g) * c, c), c) for d, g in inst]
        ni = len(inst)
        lw = [lw_s[d, rows[i], :] for i, (d, g) in enumerate(inst)]
        cl = [None] * ni
        for d in range(2):
            idx = [i for i in range(ni) if inst[i][0] == d]
            res = _mask_dot(tri[d], jnp.concatenate([lw[i] for i in idx], axis=1))
            for t, i in enumerate(idx):
                cl[i] = res[:, t * LANES:(t + 1) * LANES]
        tot = [cl[i][c - 1:c, :] if inst[i][0] == 0 else cl[i][0:1, :] for i in range(ni)]
        v = [v_s[rows[i], :] for i in range(ni)]
        kkt = [kk_s[rows[i], :] * jnp.exp(cl[i] - lw[i]) for i in range(ni)]
        rt = [r_s[rows[i], :] * jnp.exp(cl[i]) for i in range(ni)]
        x = []
        for i, (d, g) in enumerate(inst):
            e_out = jnp.exp(-cl[i])
            kh = k_s[d, rows[i], :] * e_out
            bh = b_s[d, rows[i], :] * e_out
            x.append(_nt(jnp.concatenate([kkt[i], rt[i]], axis=0),
                         jnp.concatenate([stack_heads(kh), stack_heads(bh)], axis=0)))
        a_k = [jnp.where(strict[inst[i][0]], x[i][0:c, 0:2 * c], 0.0) for i in range(ni)]
        a_b = [jnp.where(strict[inst[i][0]], x[i][0:c, 2 * c:4 * c], 0.0) for i in range(ni)]
        a_rk = [jnp.where(incl[inst[i][0]], x[i][c:2 * c, 0:2 * c], 0.0) for i in range(ni)]
        a_rb = [jnp.where(incl[inst[i][0]], x[i][c:2 * c, 2 * c:4 * c], 0.0) for i in range(ni)]

        inv = [eye_w - a for a in a_b]
        pw = [_nn(a, stack_heads(a)) for a in a_b]
        s = 2
        while s < c:
            pw_st = [stack_heads(t) for t in pw]
            if 2 * s < c:
                both = [_nn(jnp.concatenate([inv[i], pw[i]], axis=0), pw_st[i]) for i in range(ni)]
                inv = [inv[i] + both[i][0:c] for i in range(ni)]
                pw = [t[c:2 * c] for t in both]
            else:
                inv = [inv[i] + _nn(inv[i], pw_st[i]) for i in range(ni)]
            s *= 2
        v_st = [stack_heads(t) for t in v]
        av = [_nn(jnp.concatenate([a_k[i], a_rk[i]], axis=0), v_st[i]) for i in range(ni)]
        akv = [t[0:c] for t in av]
        t2 = [_nn(inv[i], stack_heads(jnp.concatenate([kkt[i], akv[i]], axis=1), head0_2)) for i in range(ni)]
        ro = [_nn(a_rb[i], stack_heads(t2[i], head0_2)) for i in range(ni)]
        rq = [rt[i] - ro[i][:, 0:LANES] for i in range(ni)]
        o_loc = [av[i][c:2 * c] - ro[i][:, LANES:2 * LANES] for i in range(ni)]
        e_end = [jnp.exp(tot[i] - cl[i]) for i in range(ni)]
        x2 = [_tn(inv[i], b_s[inst[i][0], rows[i], :] * e_end[i]) for i in range(ni)]
        w_bd = [jnp.where(same_head, t, 0.0) for t in x2]
        w_ln = [diag_blocks(t) for t in x2]
        m_kw = [_tn(stack_heads(kkt[i]), w_bd[i]) for i in range(ni)]
        aw = [diag_blocks(_tn(a_k[i], w_ln[i])) for i in range(ni)]
        n_bd = [jnp.where(same_head, _tn(v[i], k_s[inst[i][0], rows[i], :] * e_end[i] - aw[i]), 0.0)
                for i in range(ni)]

        start = [None] * ni
        for g in range(grp):
            for d in range(2):
                i = d * grp + g
                gt = gt_ref[d]
                start[i] = gt
                gt_ref[d] = gt * jnp.exp(tot[i]) - _nn(gt, m_kw[i]) + n_bd[i]
        for i in range(ni):
            o_s[rows[i], :] += _nt(rq[i], start[i]) + o_loc[i]

    for seg_first, seg_last, grp in ((0, n_cc, grp_ctx), (n_cc, n_c, grp_lat)):
        def body(it, carry, seg_first=seg_first, seg_last=seg_last, grp=grp):
            advance(seg_first + it * grp, seg_last - it * grp, grp)
            return carry

        lax.fori_loop(0, (seg_last - seg_first) // grp, body, 0)

    o = o_s[...]
    mu = _dot_mask(o, head_sum) * (1.0 / hd)
    dev = o - mu
    var = _dot_mask(dev * dev, head_sum) * (1.0 / hd)
    on = dev * lax.rsqrt(var + RWKV_GN_EPS) * vec_ref[7:8, :] + vec_ref[8:9, :]
    y_ref[...] = ((on + bonus_rk * v_s[...]) * _silu(pg_ref[...])).astype(y_ref.dtype)


def _rwkv(p, n_ctx, w, shift_w, w0, w_up, a0, a_up, k_k, k_a, r_k, ln_g, ln_b):
    bsz, n_tot, _ = p.shape
    rank = w_up.shape[1]
    n_shift = shift_w.shape[1]
    lo_w = n_shift - 3 * w
    nhp = w // LANES
    c = RWKV_CHUNK
    grp_ctx = _group(n_ctx // c, n_ctx // c, RWKV_GROUP_CTX)
    grp_lat = _group((n_tot - n_ctx) // c, (n_tot - n_ctx) // c, RWKV_GROUP_LAT)
    up = jnp.zeros((4, lo_w, w), F32)
    for j in range(2):
        up = up.at[j, j * rank:(j + 1) * rank].set(w_up[j])
        up = up.at[2 + j, (2 + j) * rank:(3 + j) * rank].set(a_up[j])
    vec = jnp.stack([w0[0], w0[1], a0[0], a0[1], k_k, k_a, r_k.reshape(w), ln_g, ln_b], axis=0)
    vec = jnp.pad(vec, ((0, 16 - vec.shape[0]), (0, 0)))

    def pspec(seg):
        return pl.BlockSpec((None, n_tot, LANES), lambda bi, hi: (bi, 0, seg * nhp + hi))

    def sspec(seg):
        return pl.BlockSpec((3, LANES), lambda bi, hi: (0, seg * nhp + hi))

    gate_blk = n_shift // LANES
    seq = pltpu.VMEM((n_tot, LANES), F32)
    seq2 = pltpu.VMEM((2, n_tot, LANES), F32)
    return pl.pallas_call(
        functools.partial(_rwkv_body, n_ctx=n_ctx, n_tot=n_tot, grp_ctx=grp_ctx, grp_lat=grp_lat),
        grid=(bsz, nhp),
        in_specs=[pspec(0), pspec(1), pspec(2),
                  pl.BlockSpec((None, n_tot, LANES), lambda bi, hi: (bi, 0, gate_blk + hi)),
                  pl.BlockSpec((None, n_tot, lo_w), lambda bi, hi: (bi, 0, 0)),
                  sspec(0), sspec(1), sspec(2),
                  pl.BlockSpec((4, lo_w, LANES), lambda bi, hi: (0, 0, hi)),
                  pl.BlockSpec((16, LANES), lambda bi, hi: (0, hi))],
        out_specs=pl.BlockSpec((None, n_tot, LANES), lambda bi, hi: (bi, 0, hi)),
        out_shape=jax.ShapeDtypeStruct((bsz, n_tot, w), BF16),
        scratch_shapes=[seq, seq, seq, seq2, seq2, seq2, seq, pltpu.VMEM((2, LANES, LANES), F32)],
        compiler_params=_cparams(2),
        name="rwkv_mixer",
    )(p, p, p, p, _lora_in(p, shift_w, w, n_ctx, rank), shift_w, shift_w, shift_w, up, vec)


def _hgrn_body(pq_ref, pf0_ref, pf1_ref, pi_ref, pg_ref, lb_ref, ng_ref, y_ref, o_ref, st_ref, *, n_ctx, n_tot, grp):
    blk = HGRN_BLOCK
    lb = lb_ref[...]
    o_ref[...] = jnp.zeros_like(o_ref)
    st_ref[...] = jnp.zeros_like(st_ref)
    tri_r = lax.broadcasted_iota(jnp.int32, (blk, blk), 0)
    tri_c = lax.broadcasted_iota(jnp.int32, (blk, blk), 1)
    tri = ((tri_r >= tri_c).astype(F32), (tri_r <= tri_c).astype(F32))
    half = blk // 2
    row = lax.broadcasted_iota(jnp.int32, (blk, LANES), 0)
    row_h = lax.broadcasted_iota(jnp.int32, (half, LANES), 0)
    pf_refs = (pf0_ref, pf1_ref)
    n_cb = n_ctx // blk
    n_b = n_tot // blk

    def body(it, carry):
        inst = [(d, g) for d in range(2) for g in range(grp)]
        ni = len(inst)
        rows = [pl.ds(pl.multiple_of(_scan_order(it, g, grp, n_cb, n_b, d == 1) * blk, blk), blk) for d, g in inst]
        q = [_silu(pq_ref[rows[i], :]) for i in range(ni)]
        f = [lb + (1.0 - lb) * jax.nn.sigmoid(pf_refs[inst[i][0]][rows[i], :]) for i in range(ni)]
        k = [1.0 - t for t in f]
        lg = [jnp.log(t) for t in f]
        v = [pi_ref[rows[i], :] for i in range(ni)]
        cl = [None] * ni
        for d in range(2):
            idx = [i for i in range(ni) if inst[i][0] == d]
            res = _mask_dot(tri[d], jnp.concatenate([lg[i] for i in idx], axis=1))
            for t, i in enumerate(idx):
                cl[i] = res[:, t * LANES:(t + 1) * LANES]
        tot = [cl[i][blk - 1:blk, :] if inst[i][0] == 0 else cl[i][0:1, :] for i in range(ni)]
        kv = [_tn(v[i], k[i] * jnp.exp(tot[i] - cl[i])) for i in range(ni)]
        o = [None] * ni
        for i in range(ni):
            rev = inst[i][0] == 1
            ref_row = half if rev else half - 1
            early = (row >= half) if rev else (row < half)
            beta = cl[i][ref_row:ref_row + 1, :]
            qn = q[i] * jnp.exp(jnp.where(early, NEG_BIG, cl[i] - beta))
            kn = k[i] * jnp.exp(jnp.where(early, beta - cl[i], NEG_BIG))
            o[i] = _nn(_nt(qn, kn), v[i])
        o_half = [[None, None] for _ in range(ni)]
        for j in range(blk):
            hj = j // half
            lo_r, hi_r = hj * half, (hj + 1) * half
            for i in range(ni):
                seen = (row_h >= j - lo_r) if inst[i][0] == 0 else (row_h <= j - lo_r)
                cl_h = cl[i][lo_r:hi_r, :]
                e = jnp.exp(jnp.where(seen, cl_h - cl[i][j:j + 1, :], NEG_BIG))
                a_j = jnp.sum(q[i][lo_r:hi_r, :] * e * k[i][j:j + 1, :], axis=-1, keepdims=True)
                t = a_j * v[i][j:j + 1, :]
                o_half[i][hj] = t if o_half[i][hj] is None else o_half[i][hj] + t
        o = [o[i] + jnp.concatenate(o_half[i], axis=0) for i in range(ni)]
        start = [None] * ni
        for g in range(grp):
            for d in range(2):
                i = d * grp + g
                st = st_ref[d]
                start[i] = st
                st_ref[d] = st * jnp.exp(tot[i]) + kv[i]
        for i in range(ni):
            o_ref[rows[i], :] += o[i] + _nt(q[i] * jnp.exp(cl[i]), start[i])
        return carry

    lax.fori_loop(0, n_b // grp, body, 0)
    o = o_ref[...]
    on = o * lax.rsqrt(jnp.mean(o * o, axis=-1, keepdims=True) + HGRN_NORM_EPS) * ng_ref[...]
    y_ref[...] = (on * _silu(pg_ref[...])).astype(y_ref.dtype)


def _hgrn(p, lb, norm_g, col0, n_ctx):
    bsz, n_tot, _ = p.shape
    w = lb.shape[-1]
    nh = w // LANES
    base = col0 // LANES
    grp = _group(n_ctx // HGRN_BLOCK, n_tot // HGRN_BLOCK, HGRN_GROUP)

    def pspec(seg):
        return pl.BlockSpec((None, n_tot, LANES), lambda bi, hi: (bi, 0, base + seg * nh + hi))

    return pl.pallas_call(
        functools.partial(_hgrn_body, n_ctx=n_ctx, n_tot=n_tot, grp=grp),
        grid=(bsz, nh),
        in_specs=[pspec(0), pspec(1), pspec(2), pspec(3), pspec(4),
                  pl.BlockSpec((1, LANES), lambda bi, hi: (0, hi)),
                  pl.BlockSpec((1, LANES), lambda bi, hi: (0, 0))],
        out_specs=pl.BlockSpec((None, n_tot, LANES), lambda bi, hi: (bi, 0, hi)),
        out_shape=jax.ShapeDtypeStruct((bsz, n_tot, w), BF16),
        scratch_shapes=[pltpu.VMEM((n_tot, LANES), F32), pltpu.VMEM((2, LANES, LANES), F32)],
        compiler_params=_cparams(2),
        name="hgrn_mixer",
    )(p, p, p, p, p, lb.reshape(1, w), norm_g.reshape(1, LANES))


def _na_layout(rows):
    qr, wr = NA_QROWS, NA_WROWS
    kinds = ((0, 0), (qr, 0), (rows - qr, rows - wr))
    start = np.zeros((3, qr), np.int64)
    lo = np.zeros((3, qr), np.int64)
    hi = np.zeros((3, qr), np.int64)
    for t, (r0, ws) in enumerate(kinds):
        for q in range(qr):
            rs = int(np.clip(r0 + q - NA_KH // 2, 0, rows - NA_KH))
            lo[t, q], hi[t, q] = rs - ws, rs - ws + NA_KH
            start[t, q] = ws - (r0 + q) + NA_KH - 1 + NA_SLAB_PAD
    assert start.min() >= 0 and (start + wr).max() <= NA_SLABS and lo.min() >= 0 and hi.max() <= wr
    return start, lo, hi


def _na_slabs(rpb):
    gw = NA_GRID_W
    col = np.arange(gw)
    cs = np.clip(col - NA_KW // 2, 0, gw - NA_KW)
    dc = np.clip(col[None, :] - col[:, None] + NA_KW - 1, 0, 2 * NA_KW - 2)
    col_ok = (col[None, :] >= cs[:, None]) & (col[None, :] < cs[:, None] + NA_KW)
    by_col = jnp.where(col_ok, rpb[..., dc], NEG_BIG)
    n_dr = by_col.shape[-3]
    wide = jnp.moveaxis(by_col, -3, -2).reshape(rpb.shape[:-2] + (gw, n_dr * gw))
    pad = [(0, 0)] * (wide.ndim - 1)
    even = jnp.pad(wide, pad + [(NA_SLAB_PAD * gw, (NA_SLABS - NA_SLAB_PAD - n_dr) * gw)], constant_values=NEG_BIG)
    odd = jnp.pad(even[..., gw:], pad + [(0, gw)], constant_values=NEG_BIG)
    return jnp.stack([even, odd], axis=-3)


def _na_body(q_ref, k_ref, v_ref, g_ref, slab_ref, y_ref, tab_ref, *, n_ctx, rows, ctx_out):
    scale = NA_HEAD_DIM ** -0.5
    gw, qr, wr = NA_GRID_W, NA_QROWS, NA_WROWS

    @pl.when(pl.program_id(1) == 0)
    def _():
        start, lo, hi = _na_layout(rows)
        lane = lax.broadcasted_iota(jnp.int32, (gw, wr * gw), 1)
        for t in range(3):
            for q in range(qr):
                st = int(start[t, q])
                src = slab_ref[st % 2, :, (st - st % 2) * gw:(st - st % 2 + wr) * gw]
                ok = (lane >= int(lo[t, q]) * gw) & (lane < int(hi[t, q]) * gw)
                tab_ref[t, q * gw:(q + 1) * gw, :] = jnp.where(ok, src, NEG_BIG)

    kc = k_ref[0:n_ctx, :]
    vc = v_ref[0:n_ctx, :]
    if ctx_out:
        s = _nt(q_ref[0:n_ctx, :], kc) * scale
        p = jnp.exp(s - jnp.max(s, axis=-1, keepdims=True))
        o = _nn(p, vc) / jnp.sum(p, axis=-1, keepdims=True)
        y_ref[0:n_ctx, :] = (o * _silu(g_ref[0:n_ctx, :])).astype(y_ref.dtype)
    else:
        y_ref[0:n_ctx, :] = jnp.zeros((n_ctx, LANES), y_ref.dtype)
    n_grp = rows // qr
    for g in range(n_grp):
        r0 = g * qr
        ws = min(max(r0 - NA_KH // 2, 0), rows - wr)
        kind = 0 if g == 0 else (2 if g == n_grp - 1 else 1)
        q0 = n_ctx + r0 * gw
        k0 = n_ctx + ws * gw
        qg = q_ref[q0:q0 + qr * gw, :]
        s_loc = _nt(qg, k_ref[k0:k0 + wr * gw, :]) * scale + tab_ref[kind]
        s_ctx = _nt(qg, kc) * scale
        m = jnp.maximum(jnp.max(s_loc, axis=-1, keepdims=True), jnp.max(s_ctx, axis=-1, keepdims=True))
        p_loc = jnp.exp(s_loc - m)
        p_ctx = jnp.exp(s_ctx - m)
        den = jnp.sum(p_loc, axis=-1, keepdims=True) + jnp.sum(p_ctx, axis=-1, keepdims=True)
        o = (_nn(p_loc, v_ref[k0:k0 + wr * gw, :]) + _nn(p_ctx, vc)) / den
        y_ref[q0:q0 + qr * gw, :] = (o * _silu(g_ref[q0:q0 + qr * gw, :])).astype(y_ref.dtype)


def _na(p, slabs, col0, w, n_ctx, ctx_out):
    bsz, n_tot, _ = p.shape
    nh = w // NA_HEAD_DIM
    rows = (n_tot - n_ctx) // NA_GRID_W
    base = col0 // LANES

    def pspec(seg):
        return pl.BlockSpec((None, n_tot, LANES), lambda hi, bi: (bi, 0, base + seg * nh + hi))

    return pl.pallas_call(
        functools.partial(_na_body, n_ctx=n_ctx, rows=rows, ctx_out=ctx_out),
        grid=(nh, bsz),
        in_specs=[pspec(0), pspec(1), pspec(2), pspec(3),
                  pl.BlockSpec((None, 2, NA_GRID_W, NA_SLABS * NA_GRID_W), lambda hi, bi: (hi, 0, 0, 0))],
        out_specs=pl.BlockSpec((None, n_tot, LANES), lambda hi, bi: (bi, 0, hi)),
        out_shape=jax.ShapeDtypeStruct((bsz, n_tot, w), BF16),
        scratch_shapes=[pltpu.VMEM((3, NA_QROWS * NA_GRID_W, NA_WROWS * NA_GRID_W), F32)],
        compiler_params=_cparams(2),
        name="nbr_attention",
    )(p, p, p, p, slabs)


def kernel(x, c, ctx, c_ctx, w_ada, b_ada, norm_g, w_in, rwkv_shift, rwkv_w0, rwkv_w_up, rwkv_a0, rwkv_a_up,
           rwkv_k_k, rwkv_k_a, rwkv_r_k, rwkv_ln_g, rwkv_ln_b, hgrn_lb_logits, hgrn_norm_g, na_rpb,
           w_branch, w_out, final_g):
    bsz, n_lat, d = x.shape
    n_ctx = ctx.shape[1]
    n_tot = n_ctx + n_lat
    depth = w_ada.shape[0]
    w = w_branch.shape[2]
    n_shift = rwkv_shift.shape[2]
    n_a = n_shift + w
    n_b = 5 * w
    n_c = 4 * w
    m = bsz * n_tot

    soft = jax.nn.softmax(hgrn_lb_logits.astype(F32), axis=0)
    lbs = jnp.cumsum(soft, axis=0) - soft[0:1]
    na_rows = n_lat // NA_GRID_W
    assert na_rows % NA_QROWS == 0 and na_rows >= NA_WROWS + NA_QROWS
    na_slabs = _na_slabs(na_rpb)
    wb_b = w_branch.astype(BF16)
    wo_b = w_out.astype(BF16)

    cond = _silu(jnp.concatenate([c, c_ctx[None]], axis=0))
    cond = jnp.pad(cond, ((0, 16 - cond.shape[0]), (0, 0))).astype(BF16)
    ada = [jnp.split(_matmul(cond, w_ada, layer=l)[:bsz + 1] + b_ada[l], 3, axis=-1) for l in range(depth)]
    zero_mod = jnp.zeros((bsz + 1, d), F32)

    xx = jnp.concatenate([ctx, x], axis=1).reshape(m, d)
    h = _prenorm(xx, norm_g[0], ada[0][1], ada[0][0], n_ctx, n_tot, bsz)
    for l in range(depth):
        last = l == depth - 1
        p2 = _matmul(h, w_in, layer=l)
        p = p2.reshape(bsz, n_tot, -1)
        ya = _rwkv(p, n_ctx, w, rwkv_shift[l], rwkv_w0[l], rwkv_w_up[l], rwkv_a0[l], rwkv_a_up[l],
                   rwkv_k_k[l], rwkv_k_a[l], rwkv_r_k[l], rwkv_ln_g[l], rwkv_ln_b[l])
        yb = _hgrn(p, lbs[l], hgrn_norm_g[l], n_a, n_ctx)
        yc = _na(p, na_slabs[l], n_a + n_b, w, n_ctx, not last)
        mg = _merge(ya.reshape(m, w), yb.reshape(m, w), yc.reshape(m, w), p2, n_a + n_b + n_c, wb_b, l)
        if last:
            xx, h = _out_proj(mg, wo_b, l, xx, ada[l][2], final_g, zero_mod, zero_mod, n_ctx, n_tot, bsz, F32)
        else:
            xx, h = _out_proj(mg, wo_b, l, xx, ada[l][2], norm_g[l + 1], ada[l + 1][1], ada[l + 1][0],
                              n_ctx, n_tot, bsz, BF16)
    return h.reshape(bsz, n_tot, d)[:, n_ctx:]
```

```python
import functools

import jax
import jax.numpy as jnp
import numpy as np
from jax import lax
from jax.experimental import pallas as pl
from jax.experimental.pallas import tpu as pltpu

F32 = jnp.float32
BF16 = jnp.bfloat16

NORM_EPS = 1e-6
RWKV_HEAD_DIM = 64
RWKV_GN_EPS = 64e-5
RWKV_CHUNK = 64
RWKV_GROUP_CTX = 4
RWKV_GROUP_LAT = 8
HGRN_HEAD_DIM = 128
HGRN_NORM_EPS = 1e-5
HGRN_BLOCK = 16
HGRN_GROUP = 8
NA_HEAD_DIM = 128
NA_GRID_W = 64
NA_KH = 8
NA_KW = 16
NA_QROWS = 4
NA_WROWS = 12
NA_SLAB_PAD = 4
NA_SLABS = 24
NEG_BIG = -1e30
LANES = 128
VMEM_LIMIT = 56 * 1024 * 1024


def _cparams(n_grid):
    return pltpu.CompilerParams(dimension_semantics=("arbitrary",) * n_grid, vmem_limit_bytes=VMEM_LIMIT)


def _nn(a, b):
    return jnp.dot(a.astype(BF16), b.astype(BF16), preferred_element_type=F32)


def _nt(a, b):
    return lax.dot_general(a.astype(BF16), b.astype(BF16), (((1,), (1,)), ((), ())), preferred_element_type=F32)


def _tn(a, b):
    return lax.dot_general(a.astype(BF16), b.astype(BF16), (((0,), (0,)), ((), ())), preferred_element_type=F32)


def _split3(x):
    hi = x.astype(BF16)
    r1 = x - hi.astype(F32)
    mid = r1.astype(BF16)
    return hi, mid, (r1 - mid.astype(F32)).astype(BF16)


def _mask_dot(mask, x):
    mb = mask.astype(BF16)
    hi, mid, lo = _split3(x)
    return (jnp.dot(mb, hi, preferred_element_type=F32) + jnp.dot(mb, mid, preferred_element_type=F32)
            + jnp.dot(mb, lo, preferred_element_type=F32))


def _dot_mask(x, mask):
    mb = mask.astype(BF16)
    hi = x.astype(BF16)
    mid = (x - hi.astype(F32)).astype(BF16)
    return jnp.dot(hi, mb, preferred_element_type=F32) + jnp.dot(mid, mb, preferred_element_type=F32)


def _silu(t):
    return t * jax.nn.sigmoid(t)


def _group(n_ctx_units, n_units, cap):
    for g in range(cap, 0, -1):
        if n_ctx_units % g == 0 and (n_units - n_ctx_units) % g == 0:
            return g
    return 1


def _scan_order(it, g, grp, n_ctx_units, n_units, reverse):
    first = it * grp
    if not reverse:
        return first + g
    return jnp.where(first < n_ctx_units, n_ctx_units - 1 - first, n_units - 1 - (first - n_ctx_units)) - g


def _mm_body(x_ref, w_ref, o_ref, wb_ref):
    @pl.when(pl.program_id(1) == 0)
    def _():
        wb_ref[...] = w_ref[...].astype(BF16)

    o_ref[...] = jnp.dot(x_ref[...], wb_ref[...], preferred_element_type=F32)


def _pick_tile(n, cap):
    best = None
    for t in range(LANES, cap + 1, LANES):
        if n % t == 0:
            best = t
    return best if best is not None else n


def _row_tile(m, cands):
    for t in cands:
        if m % t == 0:
            return t
    return m


def _matmul(x, w, layer=None):
    m, k = x.shape
    n = w.shape[-1]
    tn = min(n, 1024)
    tm = _row_tile(m, (1152, 1024, 768, 512, 256, 128))
    if layer is None:
        w_spec = pl.BlockSpec((k, tn), lambda j, i: (0, j))
    else:
        w_spec = pl.BlockSpec((None, k, tn), lambda j, i: (layer, 0, j))
    return pl.pallas_call(
        _mm_body,
        grid=(pl.cdiv(n, tn), m // tm),
        in_specs=[pl.BlockSpec((tm, k), lambda j, i: (i, 0)), w_spec],
        out_specs=pl.BlockSpec((tm, tn), lambda j, i: (i, j)),
        out_shape=jax.ShapeDtypeStruct((m, n), F32),
        scratch_shapes=[pltpu.VMEM((k, tn), BF16)],
        compiler_params=_cparams(2),
        name="dense",
    )(x, w)


def _merge_body(*refs, n_gate):
    y_refs = refs[0:3]
    g_refs = refs[3:3 + 3 * n_gate]
    w_ref, o_ref = refs[3 + 3 * n_gate:]
    acc = None
    for br in range(3):
        gate = jnp.concatenate([g_refs[br * n_gate + q][...] for q in range(n_gate)], axis=1)
        term = jax.nn.sigmoid(gate) * jnp.dot(y_refs[br][...], w_ref[br], preferred_element_type=F32)
        acc = term if acc is None else acc + term
    o_ref[...] = acc.astype(BF16)


def _merge(ya, yb, yc, p2, gate_col0, wb, layer):
    m, w = ya.shape
    d = wb.shape[-1]
    tn = _pick_tile(d, 1024)
    tm = _row_tile(m, (384, 256, 128))
    n_gate = tn // LANES
    y_spec = pl.BlockSpec((tm, w), lambda j, i: (i, 0))

    def g_spec(br, q):
        return pl.BlockSpec((tm, LANES), lambda j, i: (i, (gate_col0 + br * d) // LANES + j * n_gate + q))

    g_specs = [g_spec(br, q) for br in range(3) for q in range(n_gate)]
    return pl.pallas_call(
        functools.partial(_merge_body, n_gate=n_gate),
        grid=(d // tn, m // tm),
        in_specs=[y_spec, y_spec, y_spec] + g_specs + [pl.BlockSpec((None, 3, w, tn), lambda j, i: (layer, 0, 0, j))],
        out_specs=pl.BlockSpec((tm, tn), lambda j, i: (i, j)),
        out_shape=jax.ShapeDtypeStruct((m, d), BF16),
        compiler_params=_cparams(2),
        name="merge",
    )(ya, yb, yc, *([p2] * len(g_specs)), wb)


def _norm_mod(xn, ng_ref, sc_ref, sh_ref):
    y = xn * lax.rsqrt(jnp.mean(xn * xn, axis=-1, keepdims=True) + NORM_EPS) * ng_ref[...]
    return y * (1.0 + sc_ref[...]) + sh_ref[...]


def _prenorm_body(x_ref, ng_ref, sc_ref, sh_ref, h_ref):
    h_ref[...] = _norm_mod(x_ref[...], ng_ref, sc_ref, sh_ref).astype(h_ref.dtype)


def _out_body(m_ref, w_ref, x_ref, gt_ref, ng_ref, sc_ref, sh_ref, xo_ref, h_ref):
    xn = x_ref[...] + gt_ref[...] * jnp.dot(m_ref[...], w_ref[...], preferred_element_type=F32)
    xo_ref[...] = xn
    h_ref[...] = _norm_mod(xn, ng_ref, sc_ref, sh_ref).astype(h_ref.dtype)


def _token_rows(n_ctx, n_tot, bsz):
    tm = _row_tile(n_ctx, (256, 128, 64, 32, 16, 8))
    while (n_tot - n_ctx) % tm:
        tm //= 2
    per_b = n_tot // tm
    ctx_t = n_ctx // tm

    def mod_row(i):
        return jnp.where(i % per_b < ctx_t, bsz, i // per_b)

    return tm, mod_row


def _prenorm(xx2, ng, scale, shift, n_ctx, n_tot, bsz):
    m, d = xx2.shape
    tm, mod_row = _token_rows(n_ctx, n_tot, bsz)
    vec = pl.BlockSpec((None, 1, d), lambda i: (mod_row(i), 0, 0))
    return pl.pallas_call(
        _prenorm_body,
        grid=(m // tm,),
        in_specs=[pl.BlockSpec((tm, d), lambda i: (i, 0)), pl.BlockSpec((1, d), lambda i: (0, 0)), vec, vec],
        out_specs=pl.BlockSpec((tm, d), lambda i: (i, 0)),
        out_shape=jax.ShapeDtypeStruct((m, d), BF16),
        compiler_params=_cparams(1),
        name="prenorm",
    )(xx2, ng.reshape(1, d), scale[:, None, :], shift[:, None, :])


def _out_proj(m2, w_out_b, layer, xx2, gate, ng, scale, shift, n_ctx, n_tot, bsz, h_dtype):
    m, d = xx2.shape
    tm, mod_row = _token_rows(n_ctx, n_tot, bsz)
    vec = pl.BlockSpec((None, 1, d), lambda i: (mod_row(i), 0, 0))
    row = pl.BlockSpec((tm, d), lambda i: (i, 0))
    return pl.pallas_call(
        _out_body,
        grid=(m // tm,),
        in_specs=[row, pl.BlockSpec((None, d, d), lambda i: (layer, 0, 0)), row, vec,
                  pl.BlockSpec((1, d), lambda i: (0, 0)), vec, vec],
        out_specs=[row, row],
        out_shape=[jax.ShapeDtypeStruct((m, d), F32), jax.ShapeDtypeStruct((m, d), h_dtype)],
        compiler_params=_cparams(1),
        name="out_proj",
    )(m2, w_out_b, xx2, gate[:, None, :], ng.reshape(1, d), scale[:, None, :], shift[:, None, :])


def _token_shift(x, s_ref, n_ctx, n_tot):
    pos = lax.broadcasted_iota(jnp.int32, (n_tot, 1), 0)
    prev = jnp.where((pos == 0) | (pos == n_ctx), 0.0, pltpu.roll(x, 1, 0))
    nxt = jnp.where((pos == n_ctx - 1) | (pos == n_tot - 1), 0.0, pltpu.roll(x, n_tot - 1, 0))
    return s_ref[0:1, :] * prev + s_ref[1:2, :] * x + s_ref[2:3, :] * nxt


def _lora_in_body(lo_ref, s_ref, o_ref, *, n_ctx, n_tot, rank):
    lo = _token_shift(lo_ref[...], s_ref, n_ctx, n_tot)
    col = lax.broadcasted_iota(jnp.int32, lo.shape, 1)
    o_ref[...] = jnp.where(col < 2 * rank, jnp.tanh(lo), lo).astype(o_ref.dtype)


def _lora_in(p, shift_w, w, n_ctx, rank):
    bsz, n_tot, _ = p.shape
    lo_w = shift_w.shape[1] - 3 * w
    return pl.pallas_call(
        functools.partial(_lora_in_body, n_ctx=n_ctx, n_tot=n_tot, rank=rank),
        grid=(bsz,),
        in_specs=[pl.BlockSpec((None, n_tot, lo_w), lambda bi: (bi, 0, 3 * w // lo_w)),
                  pl.BlockSpec((3, lo_w), lambda bi: (0, 3 * w // lo_w))],
        out_specs=pl.BlockSpec((None, n_tot, lo_w), lambda bi: (bi, 0, 0)),
        out_shape=jax.ShapeDtypeStruct((bsz, n_tot, lo_w), BF16),
        compiler_params=_cparams(1),
        name="lora_in",
    )(p, shift_w)


def _rwkv_body(pr_ref, pk_ref, pv_ref, pg_ref, lo_ref, sr_ref, sk_ref, sv_ref, up_ref, vec_ref,
               y_ref, r_s, v_s, kk_s, lw_s, k_s, b_s, o_s, gt_ref, *, n_ctx, n_tot, grp_ctx, grp_lat):
    c = RWKV_CHUNK
    hd = RWKV_HEAD_DIM
    n_cc = n_ctx // c
    n_c = n_tot // c

    def shifted(p_ref, s_ref):
        return _token_shift(p_ref[...], s_ref, n_ctx, n_tot)

    lane_f = lax.broadcasted_iota(jnp.int32, (LANES, LANES), 0)
    lane_t = lax.broadcasted_iota(jnp.int32, (LANES, LANES), 1)
    same_head = ((lane_f < hd) == (lane_t < hd))
    head_sum = same_head.astype(F32)

    lo = lo_ref[...]
    r = shifted(pr_ref, sr_ref)
    k = shifted(pk_ref, sk_ref)
    r_s[...] = r
    v_s[...] = shifted(pv_ref, sv_ref)
    kk = k * vec_ref[4:5, :]
    kk = kk / jnp.maximum(jnp.sqrt(_dot_mask(kk * kk, head_sum)), 1e-12)
    kk_s[...] = kk
    k_a = vec_ref[5:6, :]
    bonus_k = None
    for j in range(2):
        wl = vec_ref[j:j + 1, :] + jnp.dot(lo, up_ref[j].astype(BF16), preferred_element_type=F32)
        al = vec_ref[2 + j:3 + j, :] + jnp.dot(lo, up_ref[2 + j].astype(BF16), preferred_element_type=F32)
        softplus = jnp.maximum(-wl, 0.0) + jnp.log(1.0 + jnp.exp(-jnp.abs(wl)))
        lw_s[j] = -jnp.exp(-softplus - 0.5)
        a = jax.nn.sigmoid(al)
        k_j = k * (1.0 + (a - 1.0) * k_a)
        k_s[j] = k_j
        b_s[j] = kk * a
        bonus_k = k_j if bonus_k is None else bonus_k + k_j
    bonus_rk = _dot_mask(r * bonus_k * vec_ref[6:7, :], head_sum)

    o_s[...] = jnp.zeros_like(o_s)
    gt_ref[...] = jnp.zeros_like(gt_ref)
    lane = lax.broadcasted_iota(jnp.int32, (c, LANES), 1)
    row = lax.broadcasted_iota(jnp.int32, (c, LANES), 0)
    col = lane % hd
    head0 = lane < hd
    lane2 = lax.broadcasted_iota(jnp.int32, (c, 2 * LANES), 1)
    head0_2 = (lane2 % LANES) < hd
    tri_r = lax.broadcasted_iota(jnp.int32, (c, c), 0)
    tri_c = lax.broadcasted_iota(jnp.int32, (c, c), 1)
    eye_w = (row == col).astype(F32)
    tri = ((tri_r >= tri_c).astype(F32), (tri_r <= tri_c).astype(F32))
    strict = (row > col, row < col)
    incl = (row >= col, row <= col)

    def stack_heads(t, h0=head0):
        return jnp.concatenate([jnp.where(h0, t, 0.0), jnp.where(h0, 0.0, t)], axis=0)

    def diag_blocks(t):
        return jnp.where(head0, t[0:c], t[c:2 * c])

    def advance(first, last, grp):
        inst = [(d, g) for d in range(2) for g in range(grp)]
        rows = [pl.ds(pl.multiple_of((last - 1 - g if d == 1 else first + g) * c, c), c) for d, g in inst]
        ni = len(inst)
        lw = [lw_s[d, rows[i], :] for i, (d, g) in enumerate(inst)]
        cl = [None] * ni
        for d in range(2):
            idx = [i for i in range(ni) if inst[i][0] == d]
            res = _mask_dot(tri[d], jnp.concatenate([lw[i] for i in idx], axis=1))
            for t, i in enumerate(idx):
                cl[i] = res[:, t * LANES:(t + 1) * LANES]
        tot = [cl[i][c - 1:c, :] if inst[i][0] == 0 else cl[i][0:1, :] for i in range(ni)]
        v = [v_s[rows[i], :] for i in range(ni)]
        kkt = [kk_s[rows[i], :] * jnp.exp(cl[i] - lw[i]) for i in range(ni)]
        rt = [r_s[rows[i], :] * jnp.exp(cl[i]) for i in range(ni)]
        x = []
        for i, (d, g) in enumerate(inst):
            e_out = jnp.exp(-cl[i])
            kh = k_s[d, rows[i], :] * e_out
            bh = b_s[d, rows[i], :] * e_out
            x.append(_nt(jnp.concatenate([kkt[i], rt[i]], axis=0),
                         jnp.concatenate([stack_heads(kh), stack_heads(bh)], axis=0)))
        a_k = [jnp.where(strict[inst[i][0]], x[i][0:c, 0:2 * c], 0.0) for i in range(ni)]
        a_b = [jnp.where(strict[inst[i][0]], x[i][0:c, 2 * c:4 * c], 0.0) for i in range(ni)]
        a_rk = [jnp.where(incl[inst[i][0]], x[i][c:2 * c, 0:2 * c], 0.0) for i in range(ni)]
        a_rb = [jnp.where(incl[inst[i][0]], x[i][c:2 * c, 2 * c:4 * c], 0.0) for i in range(ni)]

        inv = [eye_w - a for a in a_b]
        pw = [_nn(a, stack_heads(a)) for a in a_b]
        s = 2
        while s < c:
            pw_st = [stack_heads(t) for t in pw]
            if 2 * s < c:
                both = [_nn(jnp.concatenate([inv[i], pw[i]], axis=0), pw_st[i]) for i in range(ni)]
                inv = [inv[i] + both[i][0:c] for i in range(ni)]
                pw = [t[c:2 * c] for t in both]
            else:
                inv = [inv[i] + _nn(inv[i], pw_st[i]) for i in range(ni)]
            s *= 2
        v_st = [stack_heads(t) for t in v]
        av = [_nn(jnp.concatenate([a_k[i], a_rk[i]], axis=0), v_st[i]) for i in range(ni)]
        akv = [t[0:c] for t in av]
        t2 = [_nn(inv[i], stack_heads(jnp.concatenate([kkt[i], akv[i]], axis=1), head0_2)) for i in range(ni)]
        ro = [_nn(a_rb[i], stack_heads(t2[i], head0_2)) for i in range(ni)]
        rq = [rt[i] - ro[i][:, 0:LANES] for i in range(ni)]
        o_loc = [av[i][c:2 * c] - ro[i][:, LANES:2 * LANES] for i in range(ni)]
        e_end = [jnp.exp(tot[i] - cl[i]) for i in range(ni)]
        x2 = [_tn(inv[i], b_s[inst[i][0], rows[i], :] * e_end[i]) for i in range(ni)]
        w_bd = [jnp.where(same_head, t, 0.0) for t in x2]
        w_ln = [diag_blocks(t) for t in x2]
        m_kw = [_tn(stack_heads(kkt[i]), w_bd[i]) for i in range(ni)]
        aw = [diag_blocks(_tn(a_k[i], w_ln[i])) for i in range(ni)]
        n_bd = [jnp.where(same_head, _tn(v[i], k_s[inst[i][0], rows[i], :] * e_end[i] - aw[i]), 0.0)
                for i in range(ni)]

        start = [None] * ni
        for g in range(grp):
            for d in range(2):
                i = d * grp + g
                gt = gt_ref[d]
                start[i] = gt
                gt_ref[d] = gt * jnp.exp(tot[i]) - _nn(gt, m_kw[i]) + n_bd[i]
        for i in range(ni):
            o_s[rows[i], :] += _nt(rq[i], start[i]) + o_loc[i]

    for seg_first, seg_last, grp in ((0, n_cc, grp_ctx), (n_cc, n_c, grp_lat)):
        def body(it, carry, seg_first=seg_first, seg_last=seg_last, grp=grp):
            advance(seg_first + it * grp, seg_last - it * grp, grp)
            return carry

        lax.fori_loop(0, (seg_last - seg_first) // grp, body, 0)

    o = o_s[...]
    mu = _dot_mask(o, head_sum) * (1.0 / hd)
    dev = o - mu
    var = _dot_mask(dev * dev, head_sum) * (1.0 / hd)
    on = dev * lax.rsqrt(var + RWKV_GN_EPS) * vec_ref[7:8, :] + vec_ref[8:9, :]
    y_ref[...] = ((on + bonus_rk * v_s[...]) * _silu(pg_ref[...])).astype(y_ref.dtype)


def _rwkv(p, n_ctx, w, shift_w, w0, w_up, a0, a_up, k_k, k_a, r_k, ln_g, ln_b):
    bsz, n_tot, _ = p.shape
    rank = w_up.shape[1]
    n_shift = shift_w.shape[1]
    lo_w = n_shift - 3 * w
    nhp = w // LANES
    c = RWKV_CHUNK
    grp_ctx = _group(n_ctx // c, n_ctx // c, RWKV_GROUP_CTX)
    grp_lat = _group((n_tot - n_ctx) // c, (n_tot - n_ctx) // c, RWKV_GROUP_LAT)
    up = jnp.zeros((4, lo_w, w), F32)
    for j in range(2):
        up = up.at[j, j * rank:(j + 1) * rank].set(w_up[j])
        up = up.at[2 + j, (2 + j) * rank:(3 + j) * rank].set(a_up[j])
    vec = jnp.stack([w0[0], w0[1], a0[0], a0[1], k_k, k_a, r_k.reshape(w), ln_g, ln_b], axis=0)
    vec = jnp.pad(vec, ((0, 16 - vec.shape[0]), (0, 0)))

    def pspec(seg):
        return pl.BlockSpec((None, n_tot, LANES), lambda bi, hi: (bi, 0, seg * nhp + hi))

    def sspec(seg):
        return pl.BlockSpec((3, LANES), lambda bi, hi: (0, seg * nhp + hi))

    gate_blk = n_shift // LANES
    seq = pltpu.VMEM((n_tot, LANES), F32)
    seq2 = pltpu.VMEM((2, n_tot, LANES), F32)
    return pl.pallas_call(
        functools.partial(_rwkv_body, n_ctx=n_ctx, n_tot=n_tot, grp_ctx=grp_ctx, grp_lat=grp_lat),
        grid=(bsz, nhp),
        in_specs=[pspec(0), pspec(1), pspec(2),
                  pl.BlockSpec((None, n_tot, LANES), lambda bi, hi: (bi, 0, gate_blk + hi)),
                  pl.BlockSpec((None, n_tot, lo_w), lambda bi, hi: (bi, 0, 0)),
                  sspec(0), sspec(1), sspec(2),
                  pl.BlockSpec((4, lo_w, LANES), lambda bi, hi: (0, 0, hi)),
                  pl.BlockSpec((16, LANES), lambda bi, hi: (0, hi))],
        out_specs=pl.BlockSpec((None, n_tot, LANES), lambda bi, hi: (bi, 0, hi)),
        out_shape=jax.ShapeDtypeStruct((bsz, n_tot, w), BF16),
        scratch_shapes=[seq, seq, seq, seq2, seq2, seq2, seq, pltpu.VMEM((2, LANES, LANES), F32)],
        compiler_params=_cparams(2),
        name="rwkv_mixer",
    )(p, p, p, p, _lora_in(p, shift_w, w, n_ctx, rank), shift_w, shift_w, shift_w, up, vec)


def _hgrn_body(pq_ref, pf0_ref, pf1_ref, pi_ref, pg_ref, lb_ref, ng_ref, y_ref, o_ref, st_ref, *, n_ctx, n_tot, grp):
    blk = HGRN_BLOCK
    lb = lb_ref[...]
    o_ref[...] = jnp.zeros_like(o_ref)
    st_ref[...] = jnp.zeros_like(st_ref)
    tri_r = lax.broadcasted_iota(jnp.int32, (blk, blk), 0)
    tri_c = lax.broadcasted_iota(jnp.int32, (blk, blk), 1)
    tri = ((tri_r >= tri_c).astype(F32), (tri_r <= tri_c).astype(F32))
    half = blk // 2
    row = lax.broadcasted_iota(jnp.int32, (blk, LANES), 0)
    row_h = lax.broadcasted_iota(jnp.int32, (half, LANES), 0)
    col_h = lax.broadcasted_iota(jnp.int32, (half, blk), 1)
    pf_refs = (pf0_ref, pf1_ref)
    n_cb = n_ctx // blk
    n_b = n_tot // blk

    def body(it, carry):
        inst = [(d, g) for d in range(2) for g in range(grp)]
        ni = len(inst)
        rows = [pl.ds(pl.multiple_of(_scan_order(it, g, grp, n_cb, n_b, d == 1) * blk, blk), blk) for d, g in inst]
        q = [_silu(pq_ref[rows[i], :]) for i in range(ni)]
        f = [lb + (1.0 - lb) * jax.nn.sigmoid(pf_refs[inst[i][0]][rows[i], :]) for i in range(ni)]
        k = [1.0 - t for t in f]
        lg = [jnp.log2(t) for t in f]
        v = [pi_ref[rows[i], :] for i in range(ni)]
        cl = [None] * ni
        for d in range(2):
            idx = [i for i in range(ni) if inst[i][0] == d]
            res = _mask_dot(tri[d], jnp.concatenate([lg[i] for i in idx], axis=1))
            for t, i in enumerate(idx):
                cl[i] = res[:, t * LANES:(t + 1) * LANES]
        tot = [cl[i][blk - 1:blk, :] if inst[i][0] == 0 else cl[i][0:1, :] for i in range(ni)]
        kv = [_tn(v[i], k[i] * jnp.exp2(tot[i] - cl[i])) for i in range(ni)]
        ck = [cl[i] - jnp.log2(k[i]) for i in range(ni)]
        att_h = [[jnp.zeros((half, blk), F32), jnp.zeros((half, blk), F32)] for _ in range(ni)]
        for j in range(blk):
            hj = j // half
            lo_r, hi_r = hj * half, (hj + 1) * half
            for i in range(ni):
                seen = (row_h >= j - lo_r) if inst[i][0] == 0 else (row_h <= j - lo_r)
                e = jnp.exp2(jnp.where(seen, cl[i][lo_r:hi_r, :] - ck[i][j:j + 1, :], NEG_BIG))
                a_j = jnp.sum(q[i][lo_r:hi_r, :] * e, axis=-1, keepdims=True)
                att_h[i][hj] = jnp.where(col_h == j, a_j, att_h[i][hj])
        o = [None] * ni
        for i in range(ni):
            rev = inst[i][0] == 1
            ref_row = half if rev else half - 1
            early = (row >= half) if rev else (row < half)
            beta = cl[i][ref_row:ref_row + 1, :]
            qn = q[i] * jnp.exp2(jnp.where(early, NEG_BIG, cl[i] - beta))
            kn = k[i] * jnp.exp2(jnp.where(early, beta - cl[i], NEG_BIG))
            o[i] = _nn(_nt(qn, kn) + jnp.concatenate(att_h[i], axis=0), v[i])
        start = [None] * ni
        for g in range(grp):
            for d in range(2):
                i = d * grp + g
                st = st_ref[d]
                start[i] = st
                st_ref[d] = st * jnp.exp2(tot[i]) + kv[i]
        for i in range(ni):
            o_ref[rows[i], :] += o[i] + _nt(q[i] * jnp.exp2(cl[i]), start[i])
        return carry

    lax.fori_loop(0, n_b // grp, body, 0)
    o = o_ref[...]
    on = o * lax.rsqrt(jnp.mean(o * o, axis=-1, keepdims=True) + HGRN_NORM_EPS) * ng_ref[...]
    y_ref[...] = (on * _silu(pg_ref[...])).astype(y_ref.dtype)


def _hgrn(p, lb, norm_g, col0, n_ctx):
    bsz, n_tot, _ = p.shape
    w = lb.shape[-1]
    nh = w // LANES
    base = col0 // LANES
    grp = _group(n_ctx // HGRN_BLOCK, n_tot // HGRN_BLOCK, HGRN_GROUP)

    def pspec(seg):
        return pl.BlockSpec((None, n_tot, LANES), lambda bi, hi: (bi, 0, base + seg * nh + hi))

    return pl.pallas_call(
        functools.partial(_hgrn_body, n_ctx=n_ctx, n_tot=n_tot, grp=grp),
        grid=(bsz, nh),
        in_specs=[pspec(0), pspec(1), pspec(2), pspec(3), pspec(4),
                  pl.BlockSpec((1, LANES), lambda bi, hi: (0, hi)),
                  pl.BlockSpec((1, LANES), lambda bi, hi: (0, 0))],
        out_specs=pl.BlockSpec((None, n_tot, LANES), lambda bi, hi: (bi, 0, hi)),
        out_shape=jax.ShapeDtypeStruct((bsz, n_tot, w), BF16),
        scratch_shapes=[pltpu.VMEM((n_tot, LANES), F32), pltpu.VMEM((2, LANES, LANES), F32)],
        compiler_params=_cparams(2),
        name="hgrn_mixer",
    )(p, p, p, p, p, lb.reshape(1, w), norm_g.reshape(1, LANES))


def _na_layout(rows):
    qr, wr = NA_QROWS, NA_WROWS
    kinds = ((0, 0), (qr, 0), (rows - qr, rows - wr))
    start = np.zeros((3, qr), np.int64)
    lo = np.zeros((3, qr), np.int64)
    hi = np.zeros((3, qr), np.int64)
    for t, (r0, ws) in enumerate(kinds):
        for q in range(qr):
            rs = int(np.clip(r0 + q - NA_KH // 2, 0, rows - NA_KH))
            lo[t, q], hi[t, q] = rs - ws, rs - ws + NA_KH
            start[t, q] = ws - (r0 + q) + NA_KH - 1 + NA_SLAB_PAD
    assert start.min() >= 0 and (start + wr).max() <= NA_SLABS and lo.min() >= 0 and hi.max() <= wr
    return start, lo, hi


def _na_slabs(rpb):
    gw = NA_GRID_W
    col = np.arange(gw)
    cs = np.clip(col - NA_KW // 2, 0, gw - NA_KW)
    dc = np.clip(col[None, :] - col[:, None] + NA_KW - 1, 0, 2 * NA_KW - 2)
    col_ok = (col[None, :] >= cs[:, None]) & (col[None, :] < cs[:, None] + NA_KW)
    by_col = jnp.where(col_ok, rpb[..., dc], NEG_BIG)
    n_dr = by_col.shape[-3]
    wide = jnp.moveaxis(by_col, -3, -2).reshape(rpb.shape[:-2] + (gw, n_dr * gw))
    pad = [(0, 0)] * (wide.ndim - 1)
    even = jnp.pad(wide, pad + [(NA_SLAB_PAD * gw, (NA_SLABS - NA_SLAB_PAD - n_dr) * gw)], constant_values=NEG_BIG)
    odd = jnp.pad(even[..., gw:], pad + [(0, gw)], constant_values=NEG_BIG)
    return jnp.stack([even, odd], axis=-3)


def _na_body(q_ref, k_ref, v_ref, g_ref, slab_ref, y_ref, tab_ref, *, n_ctx, rows, ctx_out):
    scale = NA_HEAD_DIM ** -0.5
    gw, qr, wr = NA_GRID_W, NA_QROWS, NA_WROWS

    @pl.when(pl.program_id(1) == 0)
    def _():
        start, lo, hi = _na_layout(rows)
        lane = lax.broadcasted_iota(jnp.int32, (gw, wr * gw), 1)
        for t in range(3):
            for q in range(qr):
                st = int(start[t, q])
                src = slab_ref[st % 2, :, (st - st % 2) * gw:(st - st % 2 + wr) * gw]
                ok = (lane >= int(lo[t, q]) * gw) & (lane < int(hi[t, q]) * gw)
                tab_ref[t, q * gw:(q + 1) * gw, :] = jnp.where(ok, src, NEG_BIG)

    kc = k_ref[0:n_ctx, :]
    vc = v_ref[0:n_ctx, :]
    if ctx_out:
        s = _nt(q_ref[0:n_ctx, :], kc) * scale
        p = jnp.exp(s - jnp.max(s, axis=-1, keepdims=True))
        o = _nn(p, vc) / jnp.sum(p, axis=-1, keepdims=True)
        y_ref[0:n_ctx, :] = (o * _silu(g_ref[0:n_ctx, :])).astype(y_ref.dtype)
    else:
        y_ref[0:n_ctx, :] = jnp.zeros((n_ctx, LANES), y_ref.dtype)
    n_grp = rows // qr
    for g in range(n_grp):
        r0 = g * qr
        ws = min(max(r0 - NA_KH // 2, 0), rows - wr)
        kind = 0 if g == 0 else (2 if g == n_grp - 1 else 1)
        q0 = n_ctx + r0 * gw
        k0 = n_ctx + ws * gw
        qg = q_ref[q0:q0 + qr * gw, :]
        s_loc = _nt(qg, k_ref[k0:k0 + wr * gw, :]) * scale + tab_ref[kind]
        s_ctx = _nt(qg, kc) * scale
        m = jnp.maximum(jnp.max(s_loc, axis=-1, keepdims=True), jnp.max(s_ctx, axis=-1, keepdims=True))
        p_loc = jnp.exp(s_loc - m)
        p_ctx = jnp.exp(s_ctx - m)
        den = jnp.sum(p_loc, axis=-1, keepdims=True) + jnp.sum(p_ctx, axis=-1, keepdims=True)
        o = (_nn(p_loc, v_ref[k0:k0 + wr * gw, :]) + _nn(p_ctx, vc)) / den
        y_ref[q0:q0 + qr * gw, :] = (o * _silu(g_ref[q0:q0 + qr * gw, :])).astype(y_ref.dtype)


def _na(p, slabs, col0, w, n_ctx, ctx_out):
    bsz, n_tot, _ = p.shape
    nh = w // NA_HEAD_DIM
    rows = (n_tot - n_ctx) // NA_GRID_W
    base = col0 // LANES

    def pspec(seg):
        return pl.BlockSpec((None, n_tot, LANES), lambda hi, bi: (bi, 0, base + seg * nh + hi))

    return pl.pallas_call(
        functools.partial(_na_body, n_ctx=n_ctx, rows=rows, ctx_out=ctx_out),
        grid=(nh, bsz),
        in_specs=[pspec(0), pspec(1), pspec(2), pspec(3),
                  pl.BlockSpec((None, 2, NA_GRID_W, NA_SLABS * NA_GRID_W), lambda hi, bi: (hi, 0, 0, 0))],
        out_specs=pl.BlockSpec((None, n_tot, LANES), lambda hi, bi: (bi, 0, hi)),
        out_shape=jax.ShapeDtypeStruct((bsz, n_tot, w), BF16),
        scratch_shapes=[pltpu.VMEM((3, NA_QROWS * NA_GRID_W, NA_WROWS * NA_GRID_W), F32)],
        compiler_params=_cparams(2),
        name="nbr_attention",
    )(p, p, p, p, slabs)


def kernel(x, c, ctx, c_ctx, w_ada, b_ada, norm_g, w_in, rwkv_shift, rwkv_w0, rwkv_w_up, rwkv_a0, rwkv_a_up,
           rwkv_k_k, rwkv_k_a, rwkv_r_k, rwkv_ln_g, rwkv_ln_b, hgrn_lb_logits, hgrn_norm_g, na_rpb,
           w_branch, w_out, final_g):
    bsz, n_lat, d = x.shape
    n_ctx = ctx.shape[1]
    n_tot = n_ctx + n_lat
    depth = w_ada.shape[0]
    w = w_branch.shape[2]
    n_shift = rwkv_shift.shape[2]
    n_a = n_shift + w
    n_b = 5 * w
    n_c = 4 * w
    m = bsz * n_tot

    soft = jax.nn.softmax(hgrn_lb_logits.astype(F32), axis=0)
    lbs = jnp.cumsum(soft, axis=0) - soft[0:1]
    na_rows = n_lat // NA_GRID_W
    assert na_rows % NA_QROWS == 0 and na_rows >= NA_WROWS + NA_QROWS
    na_slabs = _na_slabs(na_rpb)
    wb_b = w_branch.astype(BF16)
    wo_b = w_out.astype(BF16)

    cond = _silu(jnp.concatenate([c, c_ctx[None]], axis=0))
    cond = jnp.pad(cond, ((0, 16 - cond.shape[0]), (0, 0))).astype(BF16)
    ada = [jnp.split(_matmul(cond, w_ada, layer=l)[:bsz + 1] + b_ada[l], 3, axis=-1) for l in range(depth)]
    zero_mod = jnp.zeros((bsz + 1, d), F32)

    xx = jnp.concatenate([ctx, x], axis=1).reshape(m, d)
    h = _prenorm(xx, norm_g[0], ada[0][1], ada[0][0], n_ctx, n_tot, bsz)
    for l in range(depth):
        last = l == depth - 1
        p2 = _matmul(h, w_in, layer=l)
        p = p2.reshape(bsz, n_tot, -1)
        ya = _rwkv(p, n_ctx, w, rwkv_shift[l], rwkv_w0[l], rwkv_w_up[l], rwkv_a0[l], rwkv_a_up[l],
                   rwkv_k_k[l], rwkv_k_a[l], rwkv_r_k[l], rwkv_ln_g[l], rwkv_ln_b[l])
        yb = _hgrn(p, lbs[l], hgrn_norm_g[l], n_a, n_ctx)
        yc = _na(p, na_slabs[l], n_a + n_b, w, n_ctx, not last)
        mg = _merge(ya.reshape(m, w), yb.reshape(m, w), yc.reshape(m, w), p2, n_a + n_b + n_c, wb_b, l)
        if last:
            xx, h = _out_proj(mg, wo_b, l, xx, ada[l][2], final_g, zero_mod, zero_mod, n_ctx, n_tot, bsz, F32)
        else:
            xx, h = _out_proj(mg, wo_b, l, xx, ada[l][2], norm_g[l + 1], ada[l + 1][1], ada[l + 1][0],
                              n_ctx, n_tot, bsz, BF16)
    return h.reshape(bsz, n_tot, d)[:, n_ctx:]
```

```python
import functools

import jax
import jax.numpy as jnp
import numpy as np
from jax import lax
from jax.experimental import pallas as pl
from jax.experimental.pallas import tpu as pltpu

F32 = jnp.float32
BF16 = jnp.bfloat16

NORM_EPS = 1e-6
RWKV_HEAD_DIM = 64
RWKV_GN_EPS = 64e-5
RWKV_CHUNK = 64
RWKV_GROUP_CTX = 4
RWKV_GROUP_LAT = 8
HGRN_HEAD_DIM = 128
HGRN_NORM_EPS = 1e-5
HGRN_BLOCK = 16
HGRN_GROUP = 16
NA_HEAD_DIM = 128
NA_GRID_W = 64
NA_KH = 8
NA_KW = 16
NA_QROWS = 4
NA_WROWS = 12
NA_SLAB_PAD = 4
NA_SLABS = 24
NEG_BIG = -1e30
LANES = 128
VMEM_LIMIT = 56 * 1024 * 1024


def _cparams(n_grid):
    return pltpu.CompilerParams(dimension_semantics=("arbitrary",) * n_grid, vmem_limit_bytes=VMEM_LIMIT)


def _nn(a, b):
    return jnp.dot(a.astype(BF16), b.astype(BF16), preferred_element_type=F32)


def _nt(a, b):
    return lax.dot_general(a.astype(BF16), b.astype(BF16), (((1,), (1,)), ((), ())), preferred_element_type=F32)


def _tn(a, b):
    return lax.dot_general(a.astype(BF16), b.astype(BF16), (((0,), (0,)), ((), ())), preferred_element_type=F32)


def _split3(x):
    hi = x.astype(BF16)
    r1 = x - hi.astype(F32)
    mid = r1.astype(BF16)
    return hi, mid, (r1 - mid.astype(F32)).astype(BF16)


def _mask_dot(mask, x):
    mb = mask.astype(BF16)
    hi, mid, lo = _split3(x)
    return (jnp.dot(mb, hi, preferred_element_type=F32) + jnp.dot(mb, mid, preferred_element_type=F32)
            + jnp.dot(mb, lo, preferred_element_type=F32))


def _dot_mask(x, mask):
    mb = mask.astype(BF16)
    hi = x.astype(BF16)
    mid = (x - hi.astype(F32)).astype(BF16)
    return jnp.dot(hi, mb, preferred_element_type=F32) + jnp.dot(mid, mb, preferred_element_type=F32)


def _silu(t):
    return t * jax.nn.sigmoid(t)


def _group(n_ctx_units, n_units, cap):
    for g in range(cap, 0, -1):
        if n_ctx_units % g == 0 and (n_units - n_ctx_units) % g == 0:
            return g
    return 1


def _scan_order(it, g, grp, n_ctx_units, n_units, reverse):
    first = it * grp
    if not reverse:
        return first + g
    return jnp.where(first < n_ctx_units, n_ctx_units - 1 - first, n_units - 1 - (first - n_ctx_units)) - g


def _mm_body(x_ref, w_ref, o_ref, wb_ref):
    @pl.when(pl.program_id(1) == 0)
    def _():
        wb_ref[...] = w_ref[...].astype(BF16)

    o_ref[...] = jnp.dot(x_ref[...], wb_ref[...], preferred_element_type=F32)


def _pick_tile(n, cap):
    best = None
    for t in range(LANES, cap + 1, LANES):
        if n % t == 0:
            best = t
    return best if best is not None else n


def _row_tile(m, cands):
    for t in cands:
        if m % t == 0:
            return t
    return m


def _matmul(x, w, layer=None):
    m, k = x.shape
    n = w.shape[-1]
    tn = min(n, 1024)
    tm = _row_tile(m, (1152, 1024, 768, 512, 256, 128))
    if layer is None:
        w_spec = pl.BlockSpec((k, tn), lambda j, i: (0, j))
    else:
        w_spec = pl.BlockSpec((None, k, tn), lambda j, i: (layer, 0, j))
    return pl.pallas_call(
        _mm_body,
        grid=(pl.cdiv(n, tn), m // tm),
        in_specs=[pl.BlockSpec((tm, k), lambda j, i: (i, 0)), w_spec],
        out_specs=pl.BlockSpec((tm, tn), lambda j, i: (i, j)),
        out_shape=jax.ShapeDtypeStruct((m, n), F32),
        scratch_shapes=[pltpu.VMEM((k, tn), BF16)],
        compiler_params=_cparams(2),
        name="dense",
    )(x, w)


def _merge_body(*refs, n_gate):
    y_refs = refs[0:3]
    g_refs = refs[3:3 + 3 * n_gate]
    w_ref, o_ref = refs[3 + 3 * n_gate:]
    acc = None
    for br in range(3):
        gate = jnp.concatenate([g_refs[br * n_gate + q][...] for q in range(n_gate)], axis=1)
        term = jax.nn.sigmoid(gate) * jnp.dot(y_refs[br][...], w_ref[br], preferred_element_type=F32)
        acc = term if acc is None else acc + term
    o_ref[...] = acc.astype(BF16)


def _merge(ya, yb, yc, p2, gate_col0, wb, layer):
    m, w = ya.shape
    d = wb.shape[-1]
    tn = _pick_tile(d, 1024)
    tm = _row_tile(m, (384, 256, 128))
    n_gate = tn // LANES
    y_spec = pl.BlockSpec((tm, w), lambda j, i: (i, 0))

    def g_spec(br, q):
        return pl.BlockSpec((tm, LANES), lambda j, i: (i, (gate_col0 + br * d) // LANES + j * n_gate + q))

    g_specs = [g_spec(br, q) for br in range(3) for q in range(n_gate)]
    return pl.pallas_call(
        functools.partial(_merge_body, n_gate=n_gate),
        grid=(d // tn, m // tm),
        in_specs=[y_spec, y_spec, y_spec] + g_specs + [pl.BlockSpec((None, 3, w, tn), lambda j, i: (layer, 0, 0, j))],
        out_specs=pl.BlockSpec((tm, tn), lambda j, i: (i, j)),
        out_shape=jax.ShapeDtypeStruct((m, d), BF16),
        compiler_params=_cparams(2),
        name="merge",
    )(ya, yb, yc, *([p2] * len(g_specs)), wb)


def _norm_mod(xn, ng_ref, sc_ref, sh_ref):
    y = xn * lax.rsqrt(jnp.mean(xn * xn, axis=-1, keepdims=True) + NORM_EPS) * ng_ref[...]
    return y * (1.0 + sc_ref[...]) + sh_ref[...]


def _prenorm_body(x_ref, ng_ref, sc_ref, sh_ref, h_ref):
    h_ref[...] = _norm_mod(x_ref[...], ng_ref, sc_ref, sh_ref).astype(h_ref.dtype)


def _out_body(m_ref, w_ref, x_ref, gt_ref, ng_ref, sc_ref, sh_ref, xo_ref, h_ref):
    xn = x_ref[...] + gt_ref[...] * jnp.dot(m_ref[...], w_ref[...], preferred_element_type=F32)
    xo_ref[...] = xn
    h_ref[...] = _norm_mod(xn, ng_ref, sc_ref, sh_ref).astype(h_ref.dtype)


def _token_rows(n_ctx, n_tot, bsz):
    tm = _row_tile(n_ctx, (256, 128, 64, 32, 16, 8))
    while (n_tot - n_ctx) % tm:
        tm //= 2
    per_b = n_tot // tm
    ctx_t = n_ctx // tm

    def mod_row(i):
        return jnp.where(i % per_b < ctx_t, bsz, i // per_b)

    return tm, mod_row


def _prenorm(xx2, ng, scale, shift, n_ctx, n_tot, bsz):
    m, d = xx2.shape
    tm, mod_row = _token_rows(n_ctx, n_tot, bsz)
    vec = pl.BlockSpec((None, 1, d), lambda i: (mod_row(i), 0, 0))
    return pl.pallas_call(
        _prenorm_body,
        grid=(m // tm,),
        in_specs=[pl.BlockSpec((tm, d), lambda i: (i, 0)), pl.BlockSpec((1, d), lambda i: (0, 0)), vec, vec],
        out_specs=pl.BlockSpec((tm, d), lambda i: (i, 0)),
        out_shape=jax.ShapeDtypeStruct((m, d), BF16),
        compiler_params=_cparams(1),
        name="prenorm",
    )(xx2, ng.reshape(1, d), scale[:, None, :], shift[:, None, :])


def _out_proj(m2, w_out_b, layer, xx2, gate, ng, scale, shift, n_ctx, n_tot, bsz, h_dtype):
    m, d = xx2.shape
    tm, mod_row = _token_rows(n_ctx, n_tot, bsz)
    vec = pl.BlockSpec((None, 1, d), lambda i: (mod_row(i), 0, 0))
    row = pl.BlockSpec((tm, d), lambda i: (i, 0))
    return pl.pallas_call(
        _out_body,
        grid=(m // tm,),
        in_specs=[row, pl.BlockSpec((None, d, d), lambda i: (layer, 0, 0)), row, vec,
                  pl.BlockSpec((1, d), lambda i: (0, 0)), vec, vec],
        out_specs=[row, row],
        out_shape=[jax.ShapeDtypeStruct((m, d), F32), jax.ShapeDtypeStruct((m, d), h_dtype)],
        compiler_params=_cparams(1),
        name="out_proj",
    )(m2, w_out_b, xx2, gate[:, None, :], ng.reshape(1, d), scale[:, None, :], shift[:, None, :])


def _token_shift(x, s_ref, n_ctx, n_tot):
    pos = lax.broadcasted_iota(jnp.int32, (n_tot, 1), 0)
    prev = jnp.where((pos == 0) | (pos == n_ctx), 0.0, pltpu.roll(x, 1, 0))
    nxt = jnp.where((pos == n_ctx - 1) | (pos == n_tot - 1), 0.0, pltpu.roll(x, n_tot - 1, 0))
    return s_ref[0:1, :] * prev + s_ref[1:2, :] * x + s_ref[2:3, :] * nxt


def _lora_in_body(lo_ref, s_ref, o_ref, *, n_ctx, n_tot, rank):
    lo = _token_shift(lo_ref[...], s_ref, n_ctx, n_tot)
    col = lax.broadcasted_iota(jnp.int32, lo.shape, 1)
    o_ref[...] = jnp.where(col < 2 * rank, jnp.tanh(lo), lo).astype(o_ref.dtype)


def _lora_in(p, shift_w, w, n_ctx, rank):
    bsz, n_tot, _ = p.shape
    lo_w = shift_w.shape[1] - 3 * w
    return pl.pallas_call(
        functools.partial(_lora_in_body, n_ctx=n_ctx, n_tot=n_tot, rank=rank),
        grid=(bsz,),
        in_specs=[pl.BlockSpec((None, n_tot, lo_w), lambda bi: (bi, 0, 3 * w // lo_w)),
                  pl.BlockSpec((3, lo_w), lambda bi: (0, 3 * w // lo_w))],
        out_specs=pl.BlockSpec((None, n_tot, lo_w), lambda bi: (bi, 0, 0)),
        out_shape=jax.ShapeDtypeStruct((bsz, n_tot, lo_w), BF16),
        compiler_params=_cparams(1),
        name="lora_in",
    )(p, shift_w)


def _rwkv_body(pr_ref, pk_ref, pv_ref, pg_ref, lo_ref, sr_ref, sk_ref, sv_ref, up_ref, vec_ref,
               y_ref, r_s, v_s, kk_s, lw_s, k_s, b_s, o_s, gt_ref, *, n_ctx, n_tot, grp_ctx, grp_lat):
    c = RWKV_CHUNK
    hd = RWKV_HEAD_DIM
    n_cc = n_ctx // c
    n_c = n_tot // c

    def shifted(p_ref, s_ref):
        return _token_shift(p_ref[...], s_ref, n_ctx, n_tot)

    lane_f = lax.broadcasted_iota(jnp.int32, (LANES, LANES), 0)
    lane_t = lax.broadcasted_iota(jnp.int32, (LANES, LANES), 1)
    same_head = ((lane_f < hd) == (lane_t < hd))
    head_sum = same_head.astype(F32)

    lo = lo_ref[...]
    r = shifted(pr_ref, sr_ref)
    k = shifted(pk_ref, sk_ref)
    r_s[...] = r
    v_s[...] = shifted(pv_ref, sv_ref)
    kk = k * vec_ref[4:5, :]
    kk = kk / jnp.maximum(jnp.sqrt(_dot_mask(kk * kk, head_sum)), 1e-12)
    kk_s[...] = kk
    k_a = vec_ref[5:6, :]
    bonus_k = None
    for j in range(2):
        wl = vec_ref[j:j + 1, :] + jnp.dot(lo, up_ref[j].astype(BF16), preferred_element_type=F32)
        al = vec_ref[2 + j:3 + j, :] + jnp.dot(lo, up_ref[2 + j].astype(BF16), preferred_element_type=F32)
        softplus = jnp.maximum(-wl, 0.0) + jnp.log(1.0 + jnp.exp(-jnp.abs(wl)))
        lw_s[j] = -jnp.exp(-softplus - 0.5)
        a = jax.nn.sigmoid(al)
        k_j = k * (1.0 + (a - 1.0) * k_a)
        k_s[j] = k_j
        b_s[j] = kk * a
        bonus_k = k_j if bonus_k is None else bonus_k + k_j
    bonus_rk = _dot_mask(r * bonus_k * vec_ref[6:7, :], head_sum)

    o_s[...] = jnp.zeros_like(o_s)
    gt_ref[...] = jnp.zeros_like(gt_ref)
    lane = lax.broadcasted_iota(jnp.int32, (c, LANES), 1)
    row = lax.broadcasted_iota(jnp.int32, (c, LANES), 0)
    col = lane % hd
    head0 = lane < hd
    lane2 = lax.broadcasted_iota(jnp.int32, (c, 2 * LANES), 1)
    head0_2 = (lane2 % LANES) < hd
    tri_r = lax.broadcasted_iota(jnp.int32, (c, c), 0)
    tri_c = lax.broadcasted_iota(jnp.int32, (c, c), 1)
    eye_w = (row == col).astype(F32)
    tri = ((tri_r >= tri_c).astype(F32), (tri_r <= tri_c).astype(F32))
    strict = (row > col, row < col)
    incl = (row >= col, row <= col)

    def stack_heads(t, h0=head0):
        return jnp.concatenate([jnp.where(h0, t, 0.0), jnp.where(h0, 0.0, t)], axis=0)

    def diag_blocks(t):
        return jnp.where(head0, t[0:c], t[c:2 * c])

    def advance(first, last, grp):
        inst = [(d, g) for d in range(2) for g in range(grp)]
        rows = [pl.ds(pl.multiple_of((last - 1 - g if d == 1 else first + g) * c, c), c) for d, g in inst]
        ni = len(inst)
        lw = [lw_s[d, rows[i], :] for i, (d, g) in enumerate(inst)]
        cl = [None] * ni
        for d in range(2):
            idx = [i for i in range(ni) if inst[i][0] == d]
            res = _mask_dot(tri[d], jnp.concatenate([lw[i] for i in idx], axis=1))
            for t, i in enumerate(idx):
                cl[i] = res[:, t * LANES:(t + 1) * LANES]
        tot = [cl[i][c - 1:c, :] if inst[i][0] == 0 else cl[i][0:1, :] for i in range(ni)]
        v = [v_s[rows[i], :] for i in range(ni)]
        kkt = [kk_s[rows[i], :] * jnp.exp(cl[i] - lw[i]) for i in range(ni)]
        rt = [r_s[rows[i], :] * jnp.exp(cl[i]) for i in range(ni)]
        x = []
        for i, (d, g) in enumerate(inst):
            e_out = jnp.exp(-cl[i])
            kh = k_s[d, rows[i], :] * e_out
            bh = b_s[d, rows[i], :] * e_out
            x.append(_nt(jnp.concatenate([kkt[i], rt[i]], axis=0),
                         jnp.concatenate([stack_heads(kh), stack_heads(bh)], axis=0)))
        a_k = [jnp.where(strict[inst[i][0]], x[i][0:c, 0:2 * c], 0.0) for i in range(ni)]
        a_b = [jnp.where(strict[inst[i][0]], x[i][0:c, 2 * c:4 * c], 0.0) for i in range(ni)]
        a_rk = [jnp.where(incl[inst[i][0]], x[i][c:2 * c, 0:2 * c], 0.0) for i in range(ni)]
        a_rb = [jnp.where(incl[inst[i][0]], x[i][c:2 * c, 2 * c:4 * c], 0.0) for i in range(ni)]

        inv = [eye_w - a for a in a_b]
        pw = [_nn(a, stack_heads(a)) for a in a_b]
        s = 2
        while s < c:
            pw_st = [stack_heads(t) for t in pw]
            if 2 * s < c:
                both = [_nn(jnp.concatenate([inv[i], pw[i]], axis=0), pw_st[i]) for i in range(ni)]
                inv = [inv[i] + both[i][0:c] for i in range(ni)]
                pw = [t[c:2 * c] for t in both]
            else:
                inv = [inv[i] + _nn(inv[i], pw_st[i]) for i in range(ni)]
            s *= 2
        v_st = [stack_heads(t) for t in v]
        av = [_nn(jnp.concatenate([a_k[i], a_rk[i]], axis=0), v_st[i]) for i in range(ni)]
        akv = [t[0:c] for t in av]
        t2 = [_nn(inv[i], stack_heads(jnp.concatenate([kkt[i], akv[i]], axis=1), head0_2)) for i in range(ni)]
        ro = [_nn(a_rb[i], stack_heads(t2[i], head0_2)) for i in range(ni)]
        rq = [rt[i] - ro[i][:, 0:LANES] for i in range(ni)]
        o_loc = [av[i][c:2 * c] - ro[i][:, LANES:2 * LANES] for i in range(ni)]
        e_end = [jnp.exp(tot[i] - cl[i]) for i in range(ni)]
        x2 = [_tn(inv[i], b_s[inst[i][0], rows[i], :] * e_end[i]) for i in range(ni)]
        w_bd = [jnp.where(same_head, t, 0.0) for t in x2]
        w_ln = [diag_blocks(t) for t in x2]
        m_kw = [_tn(stack_heads(kkt[i]), w_bd[i]) for i in range(ni)]
        aw = [diag_blocks(_tn(a_k[i], w_ln[i])) for i in range(ni)]
        n_bd = [jnp.where(same_head, _tn(v[i], k_s[inst[i][0], rows[i], :] * e_end[i] - aw[i]), 0.0)
                for i in range(ni)]

        start = [None] * ni
        for g in range(grp):
            for d in range(2):
                i = d * grp + g
                gt = gt_ref[d]
                start[i] = gt
                gt_ref[d] = gt * jnp.exp(tot[i]) - _nn(gt, m_kw[i]) + n_bd[i]
        for i in range(ni):
            o_s[rows[i], :] += _nt(rq[i], start[i]) + o_loc[i]

    for seg_first, seg_last, grp in ((0, n_cc, grp_ctx), (n_cc, n_c, grp_lat)):
        def body(it, carry, seg_first=seg_first, seg_last=seg_last, grp=grp):
            advance(seg_first + it * grp, seg_last - it * grp, grp)
            return carry

        lax.fori_loop(0, (seg_last - seg_first) // grp, body, 0)

    o = o_s[...]
    mu = _dot_mask(o, head_sum) * (1.0 / hd)
    dev = o - mu
    var = _dot_mask(dev * dev, head_sum) * (1.0 / hd)
    on = dev * lax.rsqrt(var + RWKV_GN_EPS) * vec_ref[7:8, :] + vec_ref[8:9, :]
    y_ref[...] = ((on + bonus_rk * v_s[...]) * _silu(pg_ref[...])).astype(y_ref.dtype)


def _rwkv(p, n_ctx, w, shift_w, w0, w_up, a0, a_up, k_k, k_a, r_k, ln_g, ln_b):
    bsz, n_tot, _ = p.shape
    rank = w_up.shape[1]
    n_shift = shift_w.shape[1]
    lo_w = n_shift - 3 * w
    nhp = w // LANES
    c = RWKV_CHUNK
    grp_ctx = _group(n_ctx // c, n_ctx // c, RWKV_GROUP_CTX)
    grp_lat = _group((n_tot - n_ctx) // c, (n_tot - n_ctx) // c, RWKV_GROUP_LAT)
    up = jnp.zeros((4, lo_w, w), F32)
    for j in range(2):
        up = up.at[j, j * rank:(j + 1) * rank].set(w_up[j])
        up = up.at[2 + j, (2 + j) * rank:(3 + j) * rank].set(a_up[j])
    vec = jnp.stack([w0[0], w0[1], a0[0], a0[1], k_k, k_a, r_k.reshape(w), ln_g, ln_b], axis=0)
    vec = jnp.pad(vec, ((0, 16 - vec.shape[0]), (0, 0)))

    def pspec(seg):
        return pl.BlockSpec((None, n_tot, LANES), lambda bi, hi: (bi, 0, seg * nhp + hi))

    def sspec(seg):
        return pl.BlockSpec((3, LANES), lambda bi, hi: (0, seg * nhp + hi))

    gate_blk = n_shift // LANES
    seq = pltpu.VMEM((n_tot, LANES), F32)
    seq2 = pltpu.VMEM((2, n_tot, LANES), F32)
    return pl.pallas_call(
        functools.partial(_rwkv_body, n_ctx=n_ctx, n_tot=n_tot, grp_ctx=grp_ctx, grp_lat=grp_lat),
        grid=(bsz, nhp),
        in_specs=[pspec(0), pspec(1), pspec(2),
                  pl.BlockSpec((None, n_tot, LANES), lambda bi, hi: (bi, 0, gate_blk + hi)),
                  pl.BlockSpec((None, n_tot, lo_w), lambda bi, hi: (bi, 0, 0)),
                  sspec(0), sspec(1), sspec(2),
                  pl.BlockSpec((4, lo_w, LANES), lambda bi, hi: (0, 0, hi)),
                  pl.BlockSpec((16, LANES), lambda bi, hi: (0, hi))],
        out_specs=pl.BlockSpec((None, n_tot, LANES), lambda bi, hi: (bi, 0, hi)),
        out_shape=jax.ShapeDtypeStruct((bsz, n_tot, w), BF16),
        scratch_shapes=[seq, seq, seq, seq2, seq2, seq2, seq, pltpu.VMEM((2, LANES, LANES), F32)],
        compiler_params=_cparams(2),
        name="rwkv_mixer",
    )(p, p, p, p, _lora_in(p, shift_w, w, n_ctx, rank), shift_w, shift_w, shift_w, up, vec)


def _hgrn_body(pq_ref, pf0_ref, pf1_ref, pi_ref, pg_ref, lb_ref, ng_ref, y_ref, o_ref, st_ref, *, n_ctx, n_tot, grp):
    blk = HGRN_BLOCK
    lb = lb_ref[...]
    o_ref[...] = jnp.zeros_like(o_ref)
    st_ref[...] = jnp.zeros_like(st_ref)
    tri_r = lax.broadcasted_iota(jnp.int32, (blk, blk), 0)
    tri_c = lax.broadcasted_iota(jnp.int32, (blk, blk), 1)
    tri = ((tri_r >= tri_c).astype(F32), (tri_r <= tri_c).astype(F32))
    half = blk // 2
    row = lax.broadcasted_iota(jnp.int32, (blk, LANES), 0)
    row_h = lax.broadcasted_iota(jnp.int32, (half, LANES), 0)
    col_h = lax.broadcasted_iota(jnp.int32, (half, blk), 1)
    pf_refs = (pf0_ref, pf1_ref)
    n_cb = n_ctx // blk
    n_b = n_tot // blk

    def body(it, carry):
        inst = [(d, g) for d in range(2) for g in range(grp)]
        ni = len(inst)
        rows = [pl.ds(pl.multiple_of(_scan_order(it, g, grp, n_cb, n_b, d == 1) * blk, blk), blk) for d, g in inst]
        q = [_silu(pq_ref[rows[i], :]) for i in range(ni)]
        f = [lb + (1.0 - lb) * jax.nn.sigmoid(pf_refs[inst[i][0]][rows[i], :]) for i in range(ni)]
        k = [1.0 - t for t in f]
        lg = [jnp.log2(t) for t in f]
        v = [pi_ref[rows[i], :] for i in range(ni)]
        cl = [None] * ni
        for d in range(2):
            idx = [i for i in range(ni) if inst[i][0] == d]
            res = _mask_dot(tri[d], jnp.concatenate([lg[i] for i in idx], axis=1))
            for t, i in enumerate(idx):
                cl[i] = res[:, t * LANES:(t + 1) * LANES]
        tot = [cl[i][blk - 1:blk, :] if inst[i][0] == 0 else cl[i][0:1, :] for i in range(ni)]
        kv = [_tn(v[i], k[i] * jnp.exp2(tot[i] - cl[i])) for i in range(ni)]
        ck = [cl[i] - jnp.log2(k[i]) for i in range(ni)]
        att_h = [[jnp.zeros((half, blk), F32), jnp.zeros((half, blk), F32)] for _ in range(ni)]
        for j in range(blk):
            hj = j // half
            lo_r, hi_r = hj * half, (hj + 1) * half
            for i in range(ni):
                seen = (row_h >= j - lo_r) if inst[i][0] == 0 else (row_h <= j - lo_r)
                e = jnp.exp2(jnp.where(seen, cl[i][lo_r:hi_r, :] - ck[i][j:j + 1, :], NEG_BIG))
                a_j = jnp.sum(q[i][lo_r:hi_r, :] * e, axis=-1, keepdims=True)
                att_h[i][hj] = jnp.where(col_h == j, a_j, att_h[i][hj])
        o = [None] * ni
        for i in range(ni):
            rev = inst[i][0] == 1
            ref_row = half if rev else half - 1
            early = (row >= half) if rev else (row < half)
            beta = cl[i][ref_row:ref_row + 1, :]
            qn = q[i] * jnp.exp2(jnp.where(early, NEG_BIG, cl[i] - beta))
            kn = k[i] * jnp.exp2(jnp.where(early, beta - cl[i], NEG_BIG))
            o[i] = _nn(_nt(qn, kn) + jnp.concatenate(att_h[i], axis=0), v[i])
        start = [None] * ni
        for g in range(grp):
            for d in range(2):
                i = d * grp + g
                st = st_ref[d]
                start[i] = st
                st_ref[d] = st * jnp.exp2(tot[i]) + kv[i]
        for i in range(ni):
            o_ref[rows[i], :] += o[i] + _nt(q[i] * jnp.exp2(cl[i]), start[i])
        return carry

    lax.fori_loop(0, n_b // grp, body, 0)
    o = o_ref[...]
    on = o * lax.rsqrt(jnp.mean(o * o, axis=-1, keepdims=True) + HGRN_NORM_EPS) * ng_ref[...]
    y_ref[...] = (on * _silu(pg_ref[...])).astype(y_ref.dtype)


def _hgrn(p, lb, norm_g, col0, n_ctx):
    bsz, n_tot, _ = p.shape
    w = lb.shape[-1]
    nh = w // LANES
    base = col0 // LANES
    grp = _group(n_ctx // HGRN_BLOCK, n_tot // HGRN_BLOCK, HGRN_GROUP)

    def pspec(seg):
        return pl.BlockSpec((None, n_tot, LANES), lambda bi, hi: (bi, 0, base + seg * nh + hi))

    return pl.pallas_call(
        functools.partial(_hgrn_body, n_ctx=n_ctx, n_tot=n_tot, grp=grp),
        grid=(bsz, nh),
        in_specs=[pspec(0), pspec(1), pspec(2), pspec(3), pspec(4),
                  pl.BlockSpec((1, LANES), lambda bi, hi: (0, hi)),
                  pl.BlockSpec((1, LANES), lambda bi, hi: (0, 0))],
        out_specs=pl.BlockSpec((None, n_tot, LANES), lambda bi, hi: (bi, 0, hi)),
        out_shape=jax.ShapeDtypeStruct((bsz, n_tot, w), BF16),
        scratch_shapes=[pltpu.VMEM((n_tot, LANES), F32), pltpu.VMEM((2, LANES, LANES), F32)],
        compiler_params=_cparams(2),
        name="hgrn_mixer",
    )(p, p, p, p, p, lb.reshape(1, w), norm_g.reshape(1, LANES))


def _na_layout(rows):
    qr, wr = NA_QROWS, NA_WROWS
    kinds = ((0, 0), (qr, 0), (rows - qr, rows - wr))
    start = np.zeros((3, qr), np.int64)
    lo = np.zeros((3, qr), np.int64)
    hi = np.zeros((3, qr), np.int64)
    for t, (r0, ws) in enumerate(kinds):
        for q in range(qr):
            rs = int(np.clip(r0 + q - NA_KH // 2, 0, rows - NA_KH))
            lo[t, q], hi[t, q] = rs - ws, rs - ws + NA_KH
            start[t, q] = ws - (r0 + q) + NA_KH - 1 + NA_SLAB_PAD
    assert start.min() >= 0 and (start + wr).max() <= NA_SLABS and lo.min() >= 0 and hi.max() <= wr
    return start, lo, hi


def _na_slabs(rpb):
    gw = NA_GRID_W
    col = np.arange(gw)
    cs = np.clip(col - NA_KW // 2, 0, gw - NA_KW)
    dc = np.clip(col[None, :] - col[:, None] + NA_KW - 1, 0, 2 * NA_KW - 2)
    col_ok = (col[None, :] >= cs[:, None]) & (col[None, :] < cs[:, None] + NA_KW)
    by_col = jnp.where(col_ok, rpb[..., dc], NEG_BIG)
    n_dr = by_col.shape[-3]
    wide = jnp.moveaxis(by_col, -3, -2).reshape(rpb.shape[:-2] + (gw, n_dr * gw))
    pad = [(0, 0)] * (wide.ndim - 1)
    even = jnp.pad(wide, pad + [(NA_SLAB_PAD * gw, (NA_SLABS - NA_SLAB_PAD - n_dr) * gw)], constant_values=NEG_BIG)
    odd = jnp.pad(even[..., gw:], pad + [(0, gw)], constant_values=NEG_BIG)
    return jnp.stack([even, odd], axis=-3)


def _na_body(q_ref, k_ref, v_ref, g_ref, slab_ref, y_ref, tab_ref, *, n_ctx, rows, ctx_out):
    scale = NA_HEAD_DIM ** -0.5
    gw, qr, wr = NA_GRID_W, NA_QROWS, NA_WROWS

    @pl.when(pl.program_id(1) == 0)
    def _():
        start, lo, hi = _na_layout(rows)
        lane = lax.broadcasted_iota(jnp.int32, (gw, wr * gw), 1)
        for t in range(3):
            for q in range(qr):
                st = int(start[t, q])
                src = slab_ref[st % 2, :, (st - st % 2) * gw:(st - st % 2 + wr) * gw]
                ok = (lane >= int(lo[t, q]) * gw) & (lane < int(hi[t, q]) * gw)
                tab_ref[t, q * gw:(q + 1) * gw, :] = jnp.where(ok, src, NEG_BIG)

    kc = k_ref[0:n_ctx, :]
    vc = v_ref[0:n_ctx, :]
    if ctx_out:
        s = _nt(q_ref[0:n_ctx, :], kc) * scale
        p = jnp.exp(s - jnp.max(s, axis=-1, keepdims=True))
        o = _nn(p, vc) / jnp.sum(p, axis=-1, keepdims=True)
        y_ref[0:n_ctx, :] = (o * _silu(g_ref[0:n_ctx, :])).astype(y_ref.dtype)
    else:
        y_ref[0:n_ctx, :] = jnp.zeros((n_ctx, LANES), y_ref.dtype)
    n_grp = rows // qr
    for g in range(n_grp):
        r0 = g * qr
        ws = min(max(r0 - NA_KH // 2, 0), rows - wr)
        kind = 0 if g == 0 else (2 if g == n_grp - 1 else 1)
        q0 = n_ctx + r0 * gw
        k0 = n_ctx + ws * gw
        qg = q_ref[q0:q0 + qr * gw, :]
        s_loc = _nt(qg, k_ref[k0:k0 + wr * gw, :]) * scale + tab_ref[kind]
        s_ctx = _nt(qg, kc) * scale
        m = jnp.maximum(jnp.max(s_loc, axis=-1, keepdims=True), jnp.max(s_ctx, axis=-1, keepdims=True))
        p_loc = jnp.exp(s_loc - m)
        p_ctx = jnp.exp(s_ctx - m)
        den = jnp.sum(p_loc, axis=-1, keepdims=True) + jnp.sum(p_ctx, axis=-1, keepdims=True)
        o = (_nn(p_loc, v_ref[k0:k0 + wr * gw, :]) + _nn(p_ctx, vc)) / den
        y_ref[q0:q0 + qr * gw, :] = (o * _silu(g_ref[q0:q0 + qr * gw, :])).astype(y_ref.dtype)


def _na(p, slabs, col0, w, n_ctx, ctx_out):
    bsz, n_tot, _ = p.shape
    nh = w // NA_HEAD_DIM
    rows = (n_tot - n_ctx) // NA_GRID_W
    base = col0 // LANES

    def pspec(seg):
        return pl.BlockSpec((None, n_tot, LANES), lambda hi, bi: (bi, 0, base + seg * nh + hi))

    return pl.pallas_call(
        functools.partial(_na_body, n_ctx=n_ctx, rows=rows, ctx_out=ctx_out),
        grid=(nh, bsz),
        in_specs=[pspec(0), pspec(1), pspec(2), pspec(3),
                  pl.BlockSpec((None, 2, NA_GRID_W, NA_SLABS * NA_GRID_W), lambda hi, bi: (hi, 0, 0, 0))],
        out_specs=pl.BlockSpec((None, n_tot, LANES), lambda hi, bi: (bi, 0, hi)),
        out_shape=jax.ShapeDtypeStruct((bsz, n_tot, w), BF16),
        scratch_shapes=[pltpu.VMEM((3, NA_QROWS * NA_GRID_W, NA_WROWS * NA_GRID_W), F32)],
        compiler_params=_cparams(2),
        name="nbr_attention",
    )(p, p, p, p, slabs)


def kernel(x, c, ctx, c_ctx, w_ada, b_ada, norm_g, w_in, rwkv_shift, rwkv_w0, rwkv_w_up, rwkv_a0, rwkv_a_up,
           rwkv_k_k, rwkv_k_a, rwkv_r_k, rwkv_ln_g, rwkv_ln_b, hgrn_lb_logits, hgrn_norm_g, na_rpb,
           w_branch, w_out, final_g):
    bsz, n_lat, d = x.shape
    n_ctx = ctx.shape[1]
    n_tot = n_ctx + n_lat
    depth = w_ada.shape[0]
    w = w_branch.shape[2]
    n_shift = rwkv_shift.shape[2]
    n_a = n_shift + w
    n_b = 5 * w
    n_c = 4 * w
    m = bsz * n_tot

    soft = jax.nn.softmax(hgrn_lb_logits.astype(F32), axis=0)
    lbs = jnp.cumsum(soft, axis=0) - soft[0:1]
    na_rows = n_lat // NA_GRID_W
    assert na_rows % NA_QROWS == 0 and na_rows >= NA_WROWS + NA_QROWS
    na_slabs = _na_slabs(na_rpb)
    wb_b = w_branch.astype(BF16)
    wo_b = w_out.astype(BF16)

    cond = _silu(jnp.concatenate([c, c_ctx[None]], axis=0))
    cond = jnp.pad(cond, ((0, 16 - cond.shape[0]), (0, 0))).astype(BF16)
    ada = [jnp.split(_matmul(cond, w_ada, layer=l)[:bsz + 1] + b_ada[l], 3, axis=-1) for l in range(depth)]
    zero_mod = jnp.zeros((bsz + 1, d), F32)

    xx = jnp.concatenate([ctx, x], axis=1).reshape(m, d)
    h = _prenorm(xx, norm_g[0], ada[0][1], ada[0][0], n_ctx, n_tot, bsz)
    for l in range(depth):
        last = l == depth - 1
        p2 = _matmul(h, w_in, layer=l)
        p = p2.reshape(bsz, n_tot, -1)
        ya = _rwkv(p, n_ctx, w, rwkv_shift[l], rwkv_w0[l], rwkv_w_up[l], rwkv_a0[l], rwkv_a_up[l],
                   rwkv_k_k[l], rwkv_k_a[l], rwkv_r_k[l], rwkv_ln_g[l], rwkv_ln_b[l])
        yb = _hgrn(p, lbs[l], hgrn_norm_g[l], n_a, n_ctx)
        yc = _na(p, na_slabs[l], n_a + n_b, w, n_ctx, not last)
        mg = _merge(ya.reshape(m, w), yb.reshape(m, w), yc.reshape(m, w), p2, n_a + n_b + n_c, wb_b, l)
        if last:
            xx, h = _out_proj(mg, wo_b, l, xx, ada[l][2], final_g, zero_mod, zero_mod, n_ctx, n_tot, bsz, F32)
        else:
            xx, h = _out_proj(mg, wo_b, l, xx, ada[l][2], norm_g[l + 1], ada[l + 1][1], ada[l + 1][0],
                              n_ctx, n_tot, bsz, BF16)
    return h.reshape(bsz, n_tot, d)[:, n_ctx:]
```

```python
import functools

import jax
import jax.numpy as jnp
import numpy as np
from jax import lax
from jax.experimental import pallas as pl
from jax.experimental.pallas import tpu as pltpu

F32 = jnp.float32
BF16 = jnp.bfloat16

NORM_EPS = 1e-6
RWKV_HEAD_DIM = 64
RWKV_GN_EPS = 64e-5
RWKV_CHUNK = 64
RWKV_GROUP_CTX = 4
RWKV_GROUP_LAT = 8
HGRN_HEAD_DIM = 128
HGRN_NORM_EPS = 1e-5
HGRN_BLOCK = 16
HGRN_GROUP_CTX = 16
HGRN_GROUP_LAT = 32
NA_HEAD_DIM = 128
NA_GRID_W = 64
NA_KH = 8
NA_KW = 16
NA_QROWS = 4
NA_WROWS = 12
NA_SLAB_PAD = 4
NA_SLABS = 24
NEG_BIG = -1e30
LANES = 128
VMEM_LIMIT = 56 * 1024 * 1024


def _cparams(n_grid):
    return pltpu.CompilerParams(dimension_semantics=("arbitrary",) * n_grid, vmem_limit_bytes=VMEM_LIMIT)


def _nn(a, b):
    return jnp.dot(a.astype(BF16), b.astype(BF16), preferred_element_type=F32)


def _nt(a, b):
    return lax.dot_general(a.astype(BF16), b.astype(BF16), (((1,), (1,)), ((), ())), preferred_element_type=F32)


def _tn(a, b):
    return lax.dot_general(a.astype(BF16), b.astype(BF16), (((0,), (0,)), ((), ())), preferred_element_type=F32)


def _split3(x):
    hi = x.astype(BF16)
    r1 = x - hi.astype(F32)
    mid = r1.astype(BF16)
    return hi, mid, (r1 - mid.astype(F32)).astype(BF16)


def _mask_dot(mask, x):
    mb = mask.astype(BF16)
    hi, mid, lo = _split3(x)
    return (jnp.dot(mb, hi, preferred_element_type=F32) + jnp.dot(mb, mid, preferred_element_type=F32)
            + jnp.dot(mb, lo, preferred_element_type=F32))


def _dot_mask(x, mask):
    mb = mask.astype(BF16)
    hi = x.astype(BF16)
    mid = (x - hi.astype(F32)).astype(BF16)
    return jnp.dot(hi, mb, preferred_element_type=F32) + jnp.dot(mid, mb, preferred_element_type=F32)


def _silu(t):
    return t * jax.nn.sigmoid(t)


def _group(n_ctx_units, n_units, cap):
    for g in range(cap, 0, -1):
        if n_ctx_units % g == 0 and (n_units - n_ctx_units) % g == 0:
            return g
    return 1


def _scan_order(it, g, grp, n_ctx_units, n_units, reverse):
    first = it * grp
    if not reverse:
        return first + g
    return jnp.where(first < n_ctx_units, n_ctx_units - 1 - first, n_units - 1 - (first - n_ctx_units)) - g


def _mm_body(x_ref, w_ref, o_ref, wb_ref):
    @pl.when(pl.program_id(1) == 0)
    def _():
        wb_ref[...] = w_ref[...].astype(BF16)

    o_ref[...] = jnp.dot(x_ref[...], wb_ref[...], preferred_element_type=F32)


def _pick_tile(n, cap):
    best = None
    for t in range(LANES, cap + 1, LANES):
        if n % t == 0:
            best = t
    return best if best is not None else n


def _row_tile(m, cands):
    for t in cands:
        if m % t == 0:
            return t
    return m


def _matmul(x, w, layer=None):
    m, k = x.shape
    n = w.shape[-1]
    tn = min(n, 1024)
    tm = _row_tile(m, (1536, 1152, 1024, 768, 512, 256, 128))
    if layer is None:
        w_spec = pl.BlockSpec((k, tn), lambda j, i: (0, j))
    else:
        w_spec = pl.BlockSpec((None, k, tn), lambda j, i: (layer, 0, j))
    return pl.pallas_call(
        _mm_body,
        grid=(pl.cdiv(n, tn), m // tm),
        in_specs=[pl.BlockSpec((tm, k), lambda j, i: (i, 0)), w_spec],
        out_specs=pl.BlockSpec((tm, tn), lambda j, i: (i, j)),
        out_shape=jax.ShapeDtypeStruct((m, n), F32),
        scratch_shapes=[pltpu.VMEM((k, tn), BF16)],
        compiler_params=_cparams(2),
        name="dense",
    )(x, w)


def _merge_body(*refs, n_gate):
    y_refs = refs[0:3]
    g_refs = refs[3:3 + 3 * n_gate]
    w_ref, o_ref = refs[3 + 3 * n_gate:]
    acc = None
    for br in range(3):
        gate = jnp.concatenate([g_refs[br * n_gate + q][...] for q in range(n_gate)], axis=1)
        term = jax.nn.sigmoid(gate) * jnp.dot(y_refs[br][...], w_ref[br], preferred_element_type=F32)
        acc = term if acc is None else acc + term
    o_ref[...] = acc.astype(BF16)


def _merge(ya, yb, yc, p2, gate_col0, wb, layer):
    m, w = ya.shape
    d = wb.shape[-1]
    tn = _pick_tile(d, 1024)
    tm = _row_tile(m, (384, 256, 128))
    n_gate = tn // LANES
    y_spec = pl.BlockSpec((tm, w), lambda j, i: (i, 0))

    def g_spec(br, q):
        return pl.BlockSpec((tm, LANES), lambda j, i: (i, (gate_col0 + br * d) // LANES + j * n_gate + q))

    g_specs = [g_spec(br, q) for br in range(3) for q in range(n_gate)]
    return pl.pallas_call(
        functools.partial(_merge_body, n_gate=n_gate),
        grid=(d // tn, m // tm),
        in_specs=[y_spec, y_spec, y_spec] + g_specs + [pl.BlockSpec((None, 3, w, tn), lambda j, i: (layer, 0, 0, j))],
        out_specs=pl.BlockSpec((tm, tn), lambda j, i: (i, j)),
        out_shape=jax.ShapeDtypeStruct((m, d), BF16),
        compiler_params=_cparams(2),
        name="merge",
    )(ya, yb, yc, *([p2] * len(g_specs)), wb)


def _norm_mod(xn, ng_ref, sc_ref, sh_ref):
    y = xn * lax.rsqrt(jnp.mean(xn * xn, axis=-1, keepdims=True) + NORM_EPS) * ng_ref[...]
    return y * (1.0 + sc_ref[...]) + sh_ref[...]


def _prenorm_body(x_ref, ng_ref, sc_ref, sh_ref, h_ref):
    h_ref[...] = _norm_mod(x_ref[...], ng_ref, sc_ref, sh_ref).astype(h_ref.dtype)


def _out_body(m_ref, w_ref, x_ref, gt_ref, ng_ref, sc_ref, sh_ref, xo_ref, h_ref):
    xn = x_ref[...] + gt_ref[...] * jnp.dot(m_ref[...], w_ref[...], preferred_element_type=F32)
    xo_ref[...] = xn
    h_ref[...] = _norm_mod(xn, ng_ref, sc_ref, sh_ref).astype(h_ref.dtype)


def _token_rows(n_ctx, n_tot, bsz):
    tm = _row_tile(n_ctx, (256, 128, 64, 32, 16, 8))
    while (n_tot - n_ctx) % tm:
        tm //= 2
    per_b = n_tot // tm
    ctx_t = n_ctx // tm

    def mod_row(i):
        return jnp.where(i % per_b < ctx_t, bsz, i // per_b)

    return tm, mod_row


def _prenorm(xx2, ng, scale, shift, n_ctx, n_tot, bsz):
    m, d = xx2.shape
    tm, mod_row = _token_rows(n_ctx, n_tot, bsz)
    vec = pl.BlockSpec((None, 1, d), lambda i: (mod_row(i), 0, 0))
    return pl.pallas_call(
        _prenorm_body,
        grid=(m // tm,),
        in_specs=[pl.BlockSpec((tm, d), lambda i: (i, 0)), pl.BlockSpec((1, d), lambda i: (0, 0)), vec, vec],
        out_specs=pl.BlockSpec((tm, d), lambda i: (i, 0)),
        out_shape=jax.ShapeDtypeStruct((m, d), BF16),
        compiler_params=_cparams(1),
        name="prenorm",
    )(xx2, ng.reshape(1, d), scale[:, None, :], shift[:, None, :])


def _out_proj(m2, w_out_b, layer, xx2, gate, ng, scale, shift, n_ctx, n_tot, bsz, h_dtype):
    m, d = xx2.shape
    tm, mod_row = _token_rows(n_ctx, n_tot, bsz)
    vec = pl.BlockSpec((None, 1, d), lambda i: (mod_row(i), 0, 0))
    row = pl.BlockSpec((tm, d), lambda i: (i, 0))
    return pl.pallas_call(
        _out_body,
        grid=(m // tm,),
        in_specs=[row, pl.BlockSpec((None, d, d), lambda i: (layer, 0, 0)), row, vec,
                  pl.BlockSpec((1, d), lambda i: (0, 0)), vec, vec],
        out_specs=[row, row],
        out_shape=[jax.ShapeDtypeStruct((m, d), F32), jax.ShapeDtypeStruct((m, d), h_dtype)],
        compiler_params=_cparams(1),
        name="out_proj",
    )(m2, w_out_b, xx2, gate[:, None, :], ng.reshape(1, d), scale[:, None, :], shift[:, None, :])


def _token_shift(x, s_ref, n_ctx, n_tot):
    pos = lax.broadcasted_iota(jnp.int32, (n_tot, 1), 0)
    prev = jnp.where((pos == 0) | (pos == n_ctx), 0.0, pltpu.roll(x, 1, 0))
    nxt = jnp.where((pos == n_ctx - 1) | (pos == n_tot - 1), 0.0, pltpu.roll(x, n_tot - 1, 0))
    return s_ref[0:1, :] * prev + s_ref[1:2, :] * x + s_ref[2:3, :] * nxt


def _lora_in_body(lo_ref, s_ref, o_ref, *, n_ctx, n_tot, rank):
    lo = _token_shift(lo_ref[...], s_ref, n_ctx, n_tot)
    col = lax.broadcasted_iota(jnp.int32, lo.shape, 1)
    o_ref[...] = jnp.where(col < 2 * rank, jnp.tanh(lo), lo).astype(o_ref.dtype)


def _lora_in(p, shift_w, w, n_ctx, rank):
    bsz, n_tot, _ = p.shape
    lo_w = shift_w.shape[1] - 3 * w
    return pl.pallas_call(
        functools.partial(_lora_in_body, n_ctx=n_ctx, n_tot=n_tot, rank=rank),
        grid=(bsz,),
        in_specs=[pl.BlockSpec((None, n_tot, lo_w), lambda bi: (bi, 0, 3 * w // lo_w)),
                  pl.BlockSpec((3, lo_w), lambda bi: (0, 3 * w // lo_w))],
        out_specs=pl.BlockSpec((None, n_tot, lo_w), lambda bi: (bi, 0, 0)),
        out_shape=jax.ShapeDtypeStruct((bsz, n_tot, lo_w), BF16),
        compiler_params=_cparams(1),
        name="lora_in",
    )(p, shift_w)


def _rwkv_body(pr_ref, pk_ref, pv_ref, pg_ref, lo_ref, sr_ref, sk_ref, sv_ref, up_ref, vec_ref,
               y_ref, r_s, v_s, kk_s, lw_s, k_s, b_s, o_s, gt_ref, *, n_ctx, n_tot, grp_ctx, grp_lat):
    c = RWKV_CHUNK
    hd = RWKV_HEAD_DIM
    n_cc = n_ctx // c
    n_c = n_tot // c

    def shifted(p_ref, s_ref):
        return _token_shift(p_ref[...], s_ref, n_ctx, n_tot)

    lane_f = lax.broadcasted_iota(jnp.int32, (LANES, LANES), 0)
    lane_t = lax.broadcasted_iota(jnp.int32, (LANES, LANES), 1)
    same_head = ((lane_f < hd) == (lane_t < hd))
    head_sum = same_head.astype(F32)

    lo = lo_ref[...]
    r = shifted(pr_ref, sr_ref)
    k = shifted(pk_ref, sk_ref)
    r_s[...] = r
    v_s[...] = shifted(pv_ref, sv_ref)
    kk = k * vec_ref[4:5, :]
    kk = kk / jnp.maximum(jnp.sqrt(_dot_mask(kk * kk, head_sum)), 1e-12)
    kk_s[...] = kk
    k_a = vec_ref[5:6, :]
    bonus_k = None
    for j in range(2):
        wl = vec_ref[j:j + 1, :] + jnp.dot(lo, up_ref[j].astype(BF16), preferred_element_type=F32)
        al = vec_ref[2 + j:3 + j, :] + jnp.dot(lo, up_ref[2 + j].astype(BF16), preferred_element_type=F32)
        softplus = jnp.maximum(-wl, 0.0) + jnp.log(1.0 + jnp.exp(-jnp.abs(wl)))
        lw_s[j] = -jnp.exp(-softplus - 0.5)
        a = jax.nn.sigmoid(al)
        k_j = k * (1.0 + (a - 1.0) * k_a)
        k_s[j] = k_j
        b_s[j] = kk * a
        bonus_k = k_j if bonus_k is None else bonus_k + k_j
    bonus_rk = _dot_mask(r * bonus_k * vec_ref[6:7, :], head_sum)

    o_s[...] = jnp.zeros_like(o_s)
    gt_ref[...] = jnp.zeros_like(gt_ref)
    lane = lax.broadcasted_iota(jnp.int32, (c, LANES), 1)
    row = lax.broadcasted_iota(jnp.int32, (c, LANES), 0)
    col = lane % hd
    head0 = lane < hd
    lane2 = lax.broadcasted_iota(jnp.int32, (c, 2 * LANES), 1)
    head0_2 = (lane2 % LANES) < hd
    tri_r = lax.broadcasted_iota(jnp.int32, (c, c), 0)
    tri_c = lax.broadcasted_iota(jnp.int32, (c, c), 1)
    eye_w = (row == col).astype(F32)
    tri = ((tri_r >= tri_c).astype(F32), (tri_r <= tri_c).astype(F32))
    strict = (row > col, row < col)
    incl = (row >= col, row <= col)

    def stack_heads(t, h0=head0):
        return jnp.concatenate([jnp.where(h0, t, 0.0), jnp.where(h0, 0.0, t)], axis=0)

    def diag_blocks(t):
        return jnp.where(head0, t[0:c], t[c:2 * c])

    def advance(first, last, grp):
        inst = [(d, g) for d in range(2) for g in range(grp)]
        rows = [pl.ds(pl.multiple_of((last - 1 - g if d == 1 else first + g) * c, c), c) for d, g in inst]
        ni = len(inst)
        lw = [lw_s[d, rows[i], :] for i, (d, g) in enumerate(inst)]
        cl = [None] * ni
        for d in range(2):
            idx = [i for i in range(ni) if inst[i][0] == d]
            res = _mask_dot(tri[d], jnp.concatenate([lw[i] for i in idx], axis=1))
            for t, i in enumerate(idx):
                cl[i] = res[:, t * LANES:(t + 1) * LANES]
        tot = [cl[i][c - 1:c, :] if inst[i][0] == 0 else cl[i][0:1, :] for i in range(ni)]
        v = [v_s[rows[i], :] for i in range(ni)]
        kkt = [kk_s[rows[i], :] * jnp.exp(cl[i] - lw[i]) for i in range(ni)]
        rt = [r_s[rows[i], :] * jnp.exp(cl[i]) for i in range(ni)]
        x = []
        for i, (d, g) in enumerate(inst):
            e_out = jnp.exp(-cl[i])
            kh = k_s[d, rows[i], :] * e_out
            bh = b_s[d, rows[i], :] * e_out
            x.append(_nt(jnp.concatenate([kkt[i], rt[i]], axis=0),
                         jnp.concatenate([stack_heads(kh), stack_heads(bh)], axis=0)))
        a_k = [jnp.where(strict[inst[i][0]], x[i][0:c, 0:2 * c], 0.0) for i in range(ni)]
        a_b = [jnp.where(strict[inst[i][0]], x[i][0:c, 2 * c:4 * c], 0.0) for i in range(ni)]
        a_rk = [jnp.where(incl[inst[i][0]], x[i][c:2 * c, 0:2 * c], 0.0) for i in range(ni)]
        a_rb = [jnp.where(incl[inst[i][0]], x[i][c:2 * c, 2 * c:4 * c], 0.0) for i in range(ni)]

        inv = [eye_w - a for a in a_b]
        pw = [_nn(a, stack_heads(a)) for a in a_b]
        s = 2
        while s < c:
            pw_st = [stack_heads(t) for t in pw]
            if 2 * s < c:
                both = [_nn(jnp.concatenate([inv[i], pw[i]], axis=0), pw_st[i]) for i in range(ni)]
                inv = [inv[i] + both[i][0:c] for i in range(ni)]
                pw = [t[c:2 * c] for t in both]
            else:
                inv = [inv[i] + _nn(inv[i], pw_st[i]) for i in range(ni)]
            s *= 2
        v_st = [stack_heads(t) for t in v]
        av = [_nn(jnp.concatenate([a_k[i], a_rk[i]], axis=0), v_st[i]) for i in range(ni)]
        akv = [t[0:c] for t in av]
        t2 = [_nn(inv[i], stack_heads(jnp.concatenate([kkt[i], akv[i]], axis=1), head0_2)) for i in range(ni)]
        ro = [_nn(a_rb[i], stack_heads(t2[i], head0_2)) for i in range(ni)]
        rq = [rt[i] - ro[i][:, 0:LANES] for i in range(ni)]
        o_loc = [av[i][c:2 * c] - ro[i][:, LANES:2 * LANES] for i in range(ni)]
        e_end = [jnp.exp(tot[i] - cl[i]) for i in range(ni)]
        x2 = [_tn(inv[i], b_s[inst[i][0], rows[i], :] * e_end[i]) for i in range(ni)]
        w_bd = [jnp.where(same_head, t, 0.0) for t in x2]
        w_ln = [diag_blocks(t) for t in x2]
        m_kw = [_tn(stack_heads(kkt[i]), w_bd[i]) for i in range(ni)]
        aw = [diag_blocks(_tn(a_k[i], w_ln[i])) for i in range(ni)]
        n_bd = [jnp.where(same_head, _tn(v[i], k_s[inst[i][0], rows[i], :] * e_end[i] - aw[i]), 0.0)
                for i in range(ni)]

        start = [None] * ni
        for g in range(grp):
            for d in range(2):
                i = d * grp + g
                gt = gt_ref[d]
                start[i] = gt
                gt_ref[d] = gt * jnp.exp(tot[i]) - _nn(gt, m_kw[i]) + n_bd[i]
        for i in range(ni):
            o_s[rows[i], :] += _nt(rq[i], start[i]) + o_loc[i]

    for seg_first, seg_last, grp in ((0, n_cc, grp_ctx), (n_cc, n_c, grp_lat)):
        def body(it, carry, seg_first=seg_first, seg_last=seg_last, grp=grp):
            advance(seg_first + it * grp, seg_last - it * grp, grp)
            return carry

        lax.fori_loop(0, (seg_last - seg_first) // grp, body, 0)

    o = o_s[...]
    mu = _dot_mask(o, head_sum) * (1.0 / hd)
    dev = o - mu
    var = _dot_mask(dev * dev, head_sum) * (1.0 / hd)
    on = dev * lax.rsqrt(var + RWKV_GN_EPS) * vec_ref[7:8, :] + vec_ref[8:9, :]
    y_ref[...] = ((on + bonus_rk * v_s[...]) * _silu(pg_ref[...])).astype(y_ref.dtype)


def _rwkv(p, n_ctx, w, shift_w, w0, w_up, a0, a_up, k_k, k_a, r_k, ln_g, ln_b):
    bsz, n_tot, _ = p.shape
    rank = w_up.shape[1]
    n_shift = shift_w.shape[1]
    lo_w = n_shift - 3 * w
    nhp = w // LANES
    c = RWKV_CHUNK
    grp_ctx = _group(n_ctx // c, n_ctx // c, RWKV_GROUP_CTX)
    grp_lat = _group((n_tot - n_ctx) // c, (n_tot - n_ctx) // c, RWKV_GROUP_LAT)
    up = jnp.zeros((4, lo_w, w), F32)
    for j in range(2):
        up = up.at[j, j * rank:(j + 1) * rank].set(w_up[j])
        up = up.at[2 + j, (2 + j) * rank:(3 + j) * rank].set(a_up[j])
    vec = jnp.stack([w0[0], w0[1], a0[0], a0[1], k_k, k_a, r_k.reshape(w), ln_g, ln_b], axis=0)
    vec = jnp.pad(vec, ((0, 16 - vec.shape[0]), (0, 0)))

    def pspec(seg):
        return pl.BlockSpec((None, n_tot, LANES), lambda bi, hi: (bi, 0, seg * nhp + hi))

    def sspec(seg):
        return pl.BlockSpec((3, LANES), lambda bi, hi: (0, seg * nhp + hi))

    gate_blk = n_shift // LANES
    seq = pltpu.VMEM((n_tot, LANES), F32)
    seq2 = pltpu.VMEM((2, n_tot, LANES), F32)
    return pl.pallas_call(
        functools.partial(_rwkv_body, n_ctx=n_ctx, n_tot=n_tot, grp_ctx=grp_ctx, grp_lat=grp_lat),
        grid=(bsz, nhp),
        in_specs=[pspec(0), pspec(1), pspec(2),
                  pl.BlockSpec((None, n_tot, LANES), lambda bi, hi: (bi, 0, gate_blk + hi)),
                  pl.BlockSpec((None, n_tot, lo_w), lambda bi, hi: (bi, 0, 0)),
                  sspec(0), sspec(1), sspec(2),
                  pl.BlockSpec((4, lo_w, LANES), lambda bi, hi: (0, 0, hi)),
                  pl.BlockSpec((16, LANES), lambda bi, hi: (0, hi))],
        out_specs=pl.BlockSpec((None, n_tot, LANES), lambda bi, hi: (bi, 0, hi)),
        out_shape=jax.ShapeDtypeStruct((bsz, n_tot, w), BF16),
        scratch_shapes=[seq, seq, seq, seq2, seq2, seq2, seq, pltpu.VMEM((2, LANES, LANES), F32)],
        compiler_params=_cparams(2),
        name="rwkv_mixer",
    )(p, p, p, p, _lora_in(p, shift_w, w, n_ctx, rank), shift_w, shift_w, shift_w, up, vec)


def _hgrn_body(pq_ref, pf0_ref, pf1_ref, pi_ref, pg_ref, lb_ref, ng_ref, y_ref, o_ref, st_ref, *,
               n_ctx, n_tot, grp_ctx, grp_lat):
    blk = HGRN_BLOCK
    lb = lb_ref[...]
    o_ref[...] = jnp.zeros_like(o_ref)
    st_ref[...] = jnp.zeros_like(st_ref)
    tri_r = lax.broadcasted_iota(jnp.int32, (blk, blk), 0)
    tri_c = lax.broadcasted_iota(jnp.int32, (blk, blk), 1)
    tri = ((tri_r >= tri_c).astype(F32), (tri_r <= tri_c).astype(F32))
    half = blk // 2
    row = lax.broadcasted_iota(jnp.int32, (blk, LANES), 0)
    row_h = lax.broadcasted_iota(jnp.int32, (half, LANES), 0)
    col_h = lax.broadcasted_iota(jnp.int32, (half, blk), 1)
    pf_refs = (pf0_ref, pf1_ref)
    n_cb = n_ctx // blk
    n_b = n_tot // blk

    def advance(first, last, grp):
        inst = [(d, g) for d in range(2) for g in range(grp)]
        ni = len(inst)
        rows = [pl.ds(pl.multiple_of((last - 1 - g if d == 1 else first + g) * blk, blk), blk) for d, g in inst]
        q = [_silu(pq_ref[rows[i], :]) for i in range(ni)]
        f = [lb + (1.0 - lb) * jax.nn.sigmoid(pf_refs[inst[i][0]][rows[i], :]) for i in range(ni)]
        k = [1.0 - t for t in f]
        lg = [jnp.log2(t) for t in f]
        v = [pi_ref[rows[i], :] for i in range(ni)]
        cl = [None] * ni
        for d in range(2):
            idx = [i for i in range(ni) if inst[i][0] == d]
            res = _mask_dot(tri[d], jnp.concatenate([lg[i] for i in idx], axis=1))
            for t, i in enumerate(idx):
                cl[i] = res[:, t * LANES:(t + 1) * LANES]
        tot = [cl[i][blk - 1:blk, :] if inst[i][0] == 0 else cl[i][0:1, :] for i in range(ni)]
        kv = [_tn(v[i], k[i] * jnp.exp2(tot[i] - cl[i])) for i in range(ni)]
        ck = [cl[i] - jnp.log2(k[i]) for i in range(ni)]
        att_h = [[jnp.zeros((half, blk), F32), jnp.zeros((half, blk), F32)] for _ in range(ni)]
        for j in range(blk):
            hj = j // half
            lo_r, hi_r = hj * half, (hj + 1) * half
            for i in range(ni):
                seen = (row_h >= j - lo_r) if inst[i][0] == 0 else (row_h <= j - lo_r)
                e = jnp.exp2(jnp.where(seen, cl[i][lo_r:hi_r, :] - ck[i][j:j + 1, :], NEG_BIG))
                a_j = jnp.sum(q[i][lo_r:hi_r, :] * e, axis=-1, keepdims=True)
                att_h[i][hj] = jnp.where(col_h == j, a_j, att_h[i][hj])
        o = [None] * ni
        for i in range(ni):
            rev = inst[i][0] == 1
            ref_row = half if rev else half - 1
            early = (row >= half) if rev else (row < half)
            beta = cl[i][ref_row:ref_row + 1, :]
            qn = q[i] * jnp.exp2(jnp.where(early, NEG_BIG, cl[i] - beta))
            kn = k[i] * jnp.exp2(jnp.where(early, beta - cl[i], NEG_BIG))
            o[i] = _nn(_nt(qn, kn) + jnp.concatenate(att_h[i], axis=0), v[i])
        start = [None] * ni
        for g in range(grp):
            for d in range(2):
                i = d * grp + g
                st = st_ref[d]
                start[i] = st
                st_ref[d] = st * jnp.exp2(tot[i]) + kv[i]
        for i in range(ni):
            o_ref[rows[i], :] += o[i] + _nt(q[i] * jnp.exp2(cl[i]), start[i])

    for seg_first, seg_last, grp in ((0, n_cb, grp_ctx), (n_cb, n_b, grp_lat)):
        def body(it, carry, seg_first=seg_first, seg_last=seg_last, grp=grp):
            advance(seg_first + it * grp, seg_last - it * grp, grp)
            return carry

        lax.fori_loop(0, (seg_last - seg_first) // grp, body, 0)
    o = o_ref[...]
    on = o * lax.rsqrt(jnp.mean(o * o, axis=-1, keepdims=True) + HGRN_NORM_EPS) * ng_ref[...]
    y_ref[...] = (on * _silu(pg_ref[...])).astype(y_ref.dtype)


def _hgrn(p, lb, norm_g, col0, n_ctx):
    bsz, n_tot, _ = p.shape
    w = lb.shape[-1]
    nh = w // LANES
    base = col0 // LANES
    grp_ctx = _group(n_ctx // HGRN_BLOCK, n_ctx // HGRN_BLOCK, HGRN_GROUP_CTX)
    grp_lat = _group((n_tot - n_ctx) // HGRN_BLOCK, (n_tot - n_ctx) // HGRN_BLOCK, HGRN_GROUP_LAT)

    def pspec(seg):
        return pl.BlockSpec((None, n_tot, LANES), lambda bi, hi: (bi, 0, base + seg * nh + hi))

    return pl.pallas_call(
        functools.partial(_hgrn_body, n_ctx=n_ctx, n_tot=n_tot, grp_ctx=grp_ctx, grp_lat=grp_lat),
        grid=(bsz, nh),
        in_specs=[pspec(0), pspec(1), pspec(2), pspec(3), pspec(4),
                  pl.BlockSpec((1, LANES), lambda bi, hi: (0, hi)),
                  pl.BlockSpec((1, LANES), lambda bi, hi: (0, 0))],
        out_specs=pl.BlockSpec((None, n_tot, LANES), lambda bi, hi: (bi, 0, hi)),
        out_shape=jax.ShapeDtypeStruct((bsz, n_tot, w), BF16),
        scratch_shapes=[pltpu.VMEM((n_tot, LANES), F32), pltpu.VMEM((2, LANES, LANES), F32)],
        compiler_params=_cparams(2),
        name="hgrn_mixer",
    )(p, p, p, p, p, lb.reshape(1, w), norm_g.reshape(1, LANES))


def _na_layout(rows):
    qr, wr = NA_QROWS, NA_WROWS
    kinds = ((0, 0), (qr, 0), (rows - qr, rows - wr))
    start = np.zeros((3, qr), np.int64)
    lo = np.zeros((3, qr), np.int64)
    hi = np.zeros((3, qr), np.int64)
    for t, (r0, ws) in enumerate(kinds):
        for q in range(qr):
            rs = int(np.clip(r0 + q - NA_KH // 2, 0, rows - NA_KH))
            lo[t, q], hi[t, q] = rs - ws, rs - ws + NA_KH
            start[t, q] = ws - (r0 + q) + NA_KH - 1 + NA_SLAB_PAD
    assert start.min() >= 0 and (start + wr).max() <= NA_SLABS and lo.min() >= 0 and hi.max() <= wr
    return start, lo, hi


def _na_slabs(rpb):
    gw = NA_GRID_W
    col = np.arange(gw)
    cs = np.clip(col - NA_KW // 2, 0, gw - NA_KW)
    dc = np.clip(col[None, :] - col[:, None] + NA_KW - 1, 0, 2 * NA_KW - 2)
    col_ok = (col[None, :] >= cs[:, None]) & (col[None, :] < cs[:, None] + NA_KW)
    by_col = jnp.where(col_ok, rpb[..., dc], NEG_BIG)
    n_dr = by_col.shape[-3]
    wide = jnp.moveaxis(by_col, -3, -2).reshape(rpb.shape[:-2] + (gw, n_dr * gw))
    pad = [(0, 0)] * (wide.ndim - 1)
    even = jnp.pad(wide, pad + [(NA_SLAB_PAD * gw, (NA_SLABS - NA_SLAB_PAD - n_dr) * gw)], constant_values=NEG_BIG)
    odd = jnp.pad(even[..., gw:], pad + [(0, gw)], constant_values=NEG_BIG)
    return jnp.stack([even, odd], axis=-3)


def _na_body(q_ref, k_ref, v_ref, g_ref, slab_ref, y_ref, tab_ref, *, n_ctx, rows, ctx_out):
    scale = NA_HEAD_DIM ** -0.5
    gw, qr, wr = NA_GRID_W, NA_QROWS, NA_WROWS

    @pl.when(pl.program_id(1) == 0)
    def _():
        start, lo, hi = _na_layout(rows)
        lane = lax.broadcasted_iota(jnp.int32, (gw, wr * gw), 1)
        for t in range(3):
            for q in range(qr):
                st = int(start[t, q])
                src = slab_ref[st % 2, :, (st - st % 2) * gw:(st - st % 2 + wr) * gw]
                ok = (lane >= int(lo[t, q]) * gw) & (lane < int(hi[t, q]) * gw)
                tab_ref[t, q * gw:(q + 1) * gw, :] = jnp.where(ok, src, NEG_BIG)

    kc = k_ref[0:n_ctx, :]
    vc = v_ref[0:n_ctx, :]
    if ctx_out:
        s = _nt(q_ref[0:n_ctx, :], kc) * scale
        p = jnp.exp(s - jnp.max(s, axis=-1, keepdims=True))
        o = _nn(p, vc) / jnp.sum(p, axis=-1, keepdims=True)
        y_ref[0:n_ctx, :] = (o * _silu(g_ref[0:n_ctx, :])).astype(y_ref.dtype)
    else:
        y_ref[0:n_ctx, :] = jnp.zeros((n_ctx, LANES), y_ref.dtype)
    n_grp = rows // qr
    for g in range(n_grp):
        r0 = g * qr
        ws = min(max(r0 - NA_KH // 2, 0), rows - wr)
        kind = 0 if g == 0 else (2 if g == n_grp - 1 else 1)
        q0 = n_ctx + r0 * gw
        k0 = n_ctx + ws * gw
        qg = q_ref[q0:q0 + qr * gw, :]
        s_loc = _nt(qg, k_ref[k0:k0 + wr * gw, :]) * scale + tab_ref[kind]
        s_ctx = _nt(qg, kc) * scale
        m = jnp.maximum(jnp.max(s_loc, axis=-1, keepdims=True), jnp.max(s_ctx, axis=-1, keepdims=True))
        p_loc = jnp.exp(s_loc - m)
        p_ctx = jnp.exp(s_ctx - m)
        den = jnp.sum(p_loc, axis=-1, keepdims=True) + jnp.sum(p_ctx, axis=-1, keepdims=True)
        o = (_nn(p_loc, v_ref[k0:k0 + wr * gw, :]) + _nn(p_ctx, vc)) / den
        y_ref[q0:q0 + qr * gw, :] = (o * _silu(g_ref[q0:q0 + qr * gw, :])).astype(y_ref.dtype)


def _na(p, slabs, col0, w, n_ctx, ctx_out):
    bsz, n_tot, _ = p.shape
    nh = w // NA_HEAD_DIM
    rows = (n_tot - n_ctx) // NA_GRID_W
    base = col0 // LANES

    def pspec(seg):
        return pl.BlockSpec((None, n_tot, LANES), lambda hi, bi: (bi, 0, base + seg * nh + hi))

    return pl.pallas_call(
        functools.partial(_na_body, n_ctx=n_ctx, rows=rows, ctx_out=ctx_out),
        grid=(nh, bsz),
        in_specs=[pspec(0), pspec(1), pspec(2), pspec(3),
                  pl.BlockSpec((None, 2, NA_GRID_W, NA_SLABS * NA_GRID_W), lambda hi, bi: (hi, 0, 0, 0))],
        out_specs=pl.BlockSpec((None, n_tot, LANES), lambda hi, bi: (bi, 0, hi)),
        out_shape=jax.ShapeDtypeStruct((bsz, n_tot, w), BF16),
        scratch_shapes=[pltpu.VMEM((3, NA_QROWS * NA_GRID_W, NA_WROWS * NA_GRID_W), F32)],
        compiler_params=_cparams(2),
        name="nbr_attention",
    )(p, p, p, p, slabs)


def kernel(x, c, ctx, c_ctx, w_ada, b_ada, norm_g, w_in, rwkv_shift, rwkv_w0, rwkv_w_up, rwkv_a0, rwkv_a_up,
           rwkv_k_k, rwkv_k_a, rwkv_r_k, rwkv_ln_g, rwkv_ln_b, hgrn_lb_logits, hgrn_norm_g, na_rpb,
           w_branch, w_out, final_g):
    bsz, n_lat, d = x.shape
    n_ctx = ctx.shape[1]
    n_tot = n_ctx + n_lat
    depth = w_ada.shape[0]
    w = w_branch.shape[2]
    n_shift = rwkv_shift.shape[2]
    n_a = n_shift + w
    n_b = 5 * w
    n_c = 4 * w
    m = bsz * n_tot

    soft = jax.nn.softmax(hgrn_lb_logits.astype(F32), axis=0)
    lbs = jnp.cumsum(soft, axis=0) - soft[0:1]
    na_rows = n_lat // NA_GRID_W
    assert na_rows % NA_QROWS == 0 and na_rows >= NA_WROWS + NA_QROWS
    na_slabs = _na_slabs(na_rpb)
    wb_b = w_branch.astype(BF16)
    wo_b = w_out.astype(BF16)

    cond = _silu(jnp.concatenate([c, c_ctx[None]], axis=0))
    cond = jnp.pad(cond, ((0, 16 - cond.shape[0]), (0, 0))).astype(BF16)
    ada = [jnp.split(_matmul(cond, w_ada, layer=l)[:bsz + 1] + b_ada[l], 3, axis=-1) for l in range(depth)]
    zero_mod = jnp.zeros((bsz + 1, d), F32)

    xx = jnp.concatenate([ctx, x], axis=1).reshape(m, d)
    h = _prenorm(xx, norm_g[0], ada[0][1], ada[0][0], n_ctx, n_tot, bsz)
    for l in range(depth):
        last = l == depth - 1
        p2 = _matmul(h, w_in, layer=l)
        p = p2.reshape(bsz, n_tot, -1)
        ya = _rwkv(p, n_ctx, w, rwkv_shift[l], rwkv_w0[l], rwkv_w_up[l], rwkv_a0[l], rwkv_a_up[l],
                   rwkv_k_k[l], rwkv_k_a[l], rwkv_r_k[l], rwkv_ln_g[l], rwkv_ln_b[l])
        yb = _hgrn(p, lbs[l], hgrn_norm_g[l], n_a, n_ctx)
        yc = _na(p, na_slabs[l], n_a + n_b, w, n_ctx, not last)
        mg = _merge(ya.reshape(m, w), yb.reshape(m, w), yc.reshape(m, w), p2, n_a + n_b + n_c, wb_b, l)
        if last:
            xx, h = _out_proj(mg, wo_b, l, xx, ada[l][2], final_g, zero_mod, zero_mod, n_ctx, n_tot, bsz, F32)
        else:
            xx, h = _out_proj(mg, wo_b, l, xx, ada[l][2], norm_g[l + 1], ada[l + 1][1], ada[l + 1][0],
                              n_ctx, n_tot, bsz, BF16)
    return h.reshape(bsz, n_tot, d)[:, n_ctx:]
```
